```python
import math
import jax, jax.numpy as jnp
from jax import lax
import numpy as np

D_MODEL = 2048
BATCH = 1
SEQ = 8192
DEPTH = 4
DEC_BATCH = 4
DEC_SEQ = 2048
PAST_LEN = 128

D_MIX = D_MODEL
D_FF = 4 * D_MODEL
NORM_EPS = 1e-6
CONV_WIDTH = 4
CONV_PAD = (1, 2)

LRU_HEADS = 8
LRU_HEAD_DIM = 64
D_LRU = LRU_HEADS * LRU_HEAD_DIM
LRU_C = 8.0

RWKV_HEADS = 12
RWKV_HEAD_DIM = 64
D_RWKV = RWKV_HEADS * RWKV_HEAD_DIM
RWKV_DECAY_RANK = 96
RWKV_A_RANK = 96
RWKV_GATE_RANK = 256
RWKV_GN_EPS = 64e-5

SSD_HEADS = 12
SSD_HEAD_DIM = 64
D_SSD = SSD_HEADS * SSD_HEAD_DIM
SSD_GROUPS = 4
SSD_HEADS_PER_GROUP = SSD_HEADS // SSD_GROUPS
SSD_STATE = 128
SSD_CHUNK = 128
SSD_BC = SSD_GROUPS * SSD_STATE
SSD_CONV_DIM = D_SSD + 2 * SSD_BC

LRU_COLS = 2 * D_LRU
RWKV_COLS = 3 * D_RWKV + 2 * (RWKV_DECAY_RANK + RWKV_A_RANK) + RWKV_GATE_RANK
SSD_COLS = D_SSD + SSD_CONV_DIM + 2 * SSD_HEADS
IN_COLS = LRU_COLS + RWKV_COLS + SSD_COLS

kernel_name = 'hybrid_bidir_lru_rwkv7_ssd_encoder'


def rms_norm(x, g):
    xf = x.astype(jnp.float32)
    y = xf * lax.rsqrt(jnp.mean(xf * xf, axis=-1, keepdims=True) + NORM_EPS)
    return (y * g.astype(jnp.float32)).astype(x.dtype)


def centred_dwconv(x, w, b):
    c = x.shape[-1]
    y = lax.conv_general_dilated(x, w[:, None, :].astype(x.dtype), window_strides=(1,),
                                 padding=(CONV_PAD,), dimension_numbers=('NWC', 'WIO', 'NWC'),
                                 feature_group_count=c)
    return y + b


def centred_shift_mix(p, mu):
    prev = jnp.pad(p[:, :-1], ((0, 0), (1, 0), (0, 0)))
    nxt = jnp.pad(p[:, 1:], ((0, 0), (0, 1), (0, 0)))
    return p + mu[0] * (prev - p) + mu[1] * (nxt - p)


def linear_scan(a, bx, reverse):
    def combine(left, right):
        a_l, b_l = left
        a_r, b_r = right
        return a_l * a_r, a_r * b_l + b_r
    _, h = lax.associative_scan(combine, (a, bx), axis=1, reverse=reverse)
    return h


def rglru_mixer(p, conv_w, conv_b, gate_w, gate_b, lam):
    bsz, seq, _ = p.shape
    gate_branch, xb = p[..., :D_LRU], p[..., D_LRU:]
    xc = centred_dwconv(xb, conv_w, conv_b)
    xh = xc.reshape(bsz, seq, LRU_HEADS, LRU_HEAD_DIM)
    gates = jax.nn.sigmoid(jnp.einsum('blhi,dghij->dgblhj', xh, gate_w)
                           + gate_b[:, :, None, None]).astype(jnp.float32)
    gates = gates.reshape(2, 2, bsz, seq, D_LRU)
    r, i = gates[:, 0], gates[:, 1]
    log_a = -LRU_C * r * jax.nn.softplus(-lam.astype(jnp.float32))[:, None, None, :]
    a = jnp.exp(log_a)
    bx = jnp.sqrt(-jnp.expm1(2.0 * log_a)) * (i * xc.astype(jnp.float32)[None])
    h = linear_scan(a[0], bx[0], False) + linear_scan(a[1], bx[1], True)
    return h.astype(p.dtype) * jax.nn.gelu(gate_branch)


def rwkv7_scan(r, w, k, v, z, b, reverse):
    bsz, seq, nh, n = r.shape

    def step(s, inp):
        r_t, w_t, k_t, v_t, z_t, b_t = inp
        sa = jnp.einsum('bhvk,bhk->bhv', s, z_t)
        s = s * w_t[:, :, None, :] + sa[..., None] * b_t[:, :, None, :] + v_t[..., None] * k_t[:, :, None, :]
        return s, jnp.einsum('bhvk,bhk->bhv', s, r_t)

    s0 = jnp.zeros((bsz, nh, n, n), jnp.float32)
    xs = tuple(jnp.moveaxis(t, 1, 0) for t in (r, w, k, v, z, b))
    _, y = lax.scan(step, s0, xs, reverse=reverse)
    return jnp.moveaxis(y, 0, 1)


def rwkv7_mixer(p, mu, w0, w_up, a0, a_up, g_up, k_k, k_a, r_k, ln_g, ln_b):
    bsz, seq, _ = p.shape
    f32 = jnp.float32
    heads = lambda t: t.reshape(bsz, seq, RWKV_HEADS, RWKV_HEAD_DIM)
    p = centred_shift_mix(p, mu)
    o = 0
    r = p[..., o:o + D_RWKV]; o += D_RWKV
    k = p[..., o:o + D_RWKV]; o += D_RWKV
    v = p[..., o:o + D_RWKV]; o += D_RWKV
    w_lo = p[..., o:o + 2 * RWKV_DECAY_RANK].reshape(bsz, seq, 2, RWKV_DECAY_RANK); o += 2 * RWKV_DECAY_RANK
    a_lo = p[..., o:o + 2 * RWKV_A_RANK].reshape(bsz, seq, 2, RWKV_A_RANK); o += 2 * RWKV_A_RANK
    g_lo = p[..., o:o + RWKV_GATE_RANK]
    w_pre = (w0[:, None, None] + jnp.einsum('bldr,drc->dblc', jnp.tanh(w_lo), w_up)).astype(f32)
    decay = jnp.exp(-jnp.exp(-jax.nn.softplus(-w_pre) - 0.5))
    a = jax.nn.sigmoid((a0[:, None, None] + jnp.einsum('bldr,drc->dblc', a_lo, a_up)).astype(f32))
    g = jax.nn.sigmoid(g_lo) @ g_up
    kk = heads((k * k_k).astype(f32))
    kk = kk / jnp.maximum(jnp.sqrt(jnp.sum(kk * kk, axis=-1, keepdims=True)), 1e-12)
    k_dir = k.astype(f32)[None] * (1.0 + (a - 1.0) * k_a.astype(f32))
    rh = heads(r.astype(f32))
    vh = heads(v.astype(f32))
    y = (rwkv7_scan(rh, heads(decay[0]), heads(k_dir[0]), vh, -kk, kk * heads(a[0]), False)
         + rwkv7_scan(rh, heads(decay[1]), heads(k_dir[1]), vh, -kk, kk * heads(a[1]), True))
    mean = jnp.mean(y, axis=-1, keepdims=True)
    var = jnp.mean(jnp.square(y - mean), axis=-1, keepdims=True)
    y = ((y - mean) * lax.rsqrt(var + RWKV_GN_EPS)).reshape(bsz, seq, D_RWKV)
    y = y * ln_g.astype(f32) + ln_b.astype(f32)
    bonus = jnp.sum(rh * heads(k_dir[0] + k_dir[1]) * r_k.astype(f32), axis=-1, keepdims=True) * vh
    return (y + bonus.reshape(bsz, seq, D_RWKV)).astype(p.dtype) * g


def ssd_chunked(x, dt, a_head, bm, cm):
    bsz, seq, nh, hp = x.shape
    g, r, q = SSD_GROUPS, SSD_HEADS_PER_GROUP, SSD_CHUNK
    nc = seq // q
    xd = (x * dt[..., None]).reshape(bsz, nc, q, g, r, hp)
    a_cum = jnp.cumsum((dt * a_head).reshape(bsz, nc, q, g, r), axis=2)
    bc = bm.reshape(bsz, nc, q, g, SSD_STATE)
    cc = cm.reshape(bsz, nc, q, g, SSD_STATE)
    a_t = jnp.moveaxis(a_cum, 2, -1)
    seg = a_t[..., :, None] - a_t[..., None, :]
    lower = jnp.tril(jnp.ones((q, q), dtype=bool))
    decay_ij = jnp.exp(jnp.where(lower, seg, -jnp.inf))
    cb = jnp.einsum('bcign,bcjgn->bcgij', cc, bc)
    y_diag = jnp.einsum('bcgij,bcgrij,bcjgrp->bcigrp', cb, decay_ij, xd)
    decay_to_end = jnp.exp(a_cum[:, :, -1:] - a_cum)
    chunk_states = jnp.einsum('bcjgn,bcjgr,bcjgrp->bcgrpn', bc, decay_to_end, xd)
    chunk_decay = jnp.exp(a_cum[:, :, -1])

    def carry(h, inp):
        s_c, d_c = inp
        return h * d_c[..., None, None] + s_c, h

    h0 = jnp.zeros((bsz, g, r, hp, SSD_STATE), jnp.float32)
    _, h_in = lax.scan(carry, h0, (jnp.moveaxis(chunk_states, 1, 0), jnp.moveaxis(chunk_decay, 1, 0)))
    h_in = jnp.moveaxis(h_in, 0, 1)
    y_off = jnp.einsum('bcign,bcgrpn,bcigr->bcigrp', cc, h_in, jnp.exp(a_cum))
    return (y_diag + y_off).reshape(bsz, seq, nh, hp)


def ssd_mixer(p, conv_w, conv_b, dt_bias, a_log, d_skip, norm_g):
    bsz, seq, _ = p.shape
    f32 = jnp.float32
    z = p[..., :D_SSD]
    xbc = p[..., D_SSD:D_SSD + SSD_CONV_DIM]
    dt_raw = p[..., D_SSD + SSD_CONV_DIM:].reshape(bsz, seq, 2, SSD_HEADS)
    xbc = jax.nn.silu(centred_dwconv(xbc, conv_w, conv_b)).astype(f32)
    xs = xbc[..., :D_SSD].reshape(bsz, seq, SSD_HEADS, SSD_HEAD_DIM)
    bm = xbc[..., D_SSD:D_SSD + SSD_BC].reshape(bsz, seq, SSD_GROUPS, SSD_STATE)
    cm = xbc[..., D_SSD + SSD_BC:].reshape(bsz, seq, SSD_GROUPS, SSD_STATE)
    dt = jax.nn.softplus(dt_raw.astype(f32) + dt_bias.astype(f32))
    a_head = -jnp.exp(a_log.astype(f32))
    flip = lambda t: jnp.flip(t, axis=1)
    y_f = ssd_chunked(xs, dt[:, :, 0], a_head[0], bm, cm)
    y_b = flip(ssd_chunked(flip(xs), flip(dt[:, :, 1]), a_head[1], flip(bm), flip(cm)))
    y = y_f + y_b + d_skip.astype(f32)[:, None] * xs
    y = y.reshape(bsz, seq, D_SSD).astype(p.dtype)
    return rms_norm(y * jax.nn.silu(z), norm_g)


def encoder_trunk(x, w):
    (norm1_g, w_in, lru_conv_w, lru_conv_b, lru_gate_w, lru_gate_b, lru_lambda,
     rwkv_mu, rwkv_w0, rwkv_w_up, rwkv_a0, rwkv_a_up, rwkv_g_up, rwkv_k_k, rwkv_k_a,
     rwkv_r_k, rwkv_ln_g, rwkv_ln_b, ssd_conv_w, ssd_conv_b, ssd_dt_bias, ssd_a_log,
     ssd_d, ssd_norm_g, w_out, norm2_g, mlp_w1, mlp_w2, final_norm_g) = w
    for l in range(DEPTH):
        u = rms_norm(x, norm1_g[l])
        p = u @ w_in[l]
        y_lru = rglru_mixer(p[..., :LRU_COLS], lru_conv_w[l], lru_conv_b[l],
                            lru_gate_w[l], lru_gate_b[l], lru_lambda[l])
        y_rwkv = rwkv7_mixer(p[..., LRU_COLS:LRU_COLS + RWKV_COLS], rwkv_mu[l], rwkv_w0[l],
                             rwkv_w_up[l], rwkv_a0[l], rwkv_a_up[l], rwkv_g_up[l], rwkv_k_k[l],
                             rwkv_k_a[l], rwkv_r_k[l], rwkv_ln_g[l], rwkv_ln_b[l])
        y_ssd = ssd_mixer(p[..., LRU_COLS + RWKV_COLS:], ssd_conv_w[l], ssd_conv_b[l],
                          ssd_dt_bias[l], ssd_a_log[l], ssd_d[l], ssd_norm_g[l])
        x = x + jnp.concatenate([y_lru, y_rwkv, y_ssd], axis=-1) @ w_out[l]
        u = rms_norm(x, norm2_g[l])
        x = x + jnp.square(jax.nn.relu(u @ mlp_w1[l])) @ mlp_w2[l]
    return rms_norm(x, final_norm_g)


def setup_inputs(seed: int = 0) -> dict:
    key = jax.random.key(seed)
    ks = iter(jax.random.split(key, 48))
    f32 = jnp.float32
    L_ = DEPTH

    def nrm(shape, scale):
        return scale * jax.random.normal(next(ks), shape, f32)

    def unif(shape, lo, hi):
        return jax.random.uniform(next(ks), shape, f32, lo, hi)

    x_prompt = nrm((BATCH, SEQ, D_MODEL), 1.0)
    x_sample = nrm((DEC_BATCH, DEC_SEQ, D_MODEL), 1.0)
    lru_s = unif((L_, 2, D_LRU), 0.9, 0.999) ** (1.0 / LRU_C)
    lru_lambda = jnp.log(lru_s) - jnp.log1p(-lru_s)
    dt0 = jnp.exp(unif((L_, 2, SSD_HEADS), math.log(1e-3), math.log(1e-1)))
    ssd_dt_bias = dt0 + jnp.log(-jnp.expm1(-dt0))
    return {
        'x_prompt': x_prompt,
        'x_sample': x_sample,
        'norm1_g': 1.0 + nrm((L_, D_MODEL), 0.02),
        'w_in': nrm((L_, D_MODEL, IN_COLS), D_MODEL ** -0.5),
        'lru_conv_w': nrm((L_, CONV_WIDTH, D_LRU), CONV_WIDTH ** -0.5),
        'lru_conv_b': nrm((L_, D_LRU), 0.02),
        'lru_gate_w': nrm((L_, 2, 2, LRU_HEADS, LRU_HEAD_DIM, LRU_HEAD_DIM), LRU_HEAD_DIM ** -0.5),
        'lru_gate_b': nrm((L_, 2, 2, LRU_HEADS, LRU_HEAD_DIM), 0.02),
        'lru_lambda': lru_lambda,
        'rwkv_mu': unif((L_, 2, RWKV_COLS), 0.0, 0.4),
        'rwkv_w0': unif((L_, 2, D_RWKV), -6.0, 1.0),
        'rwkv_w_up': nrm((L_, 2, RWKV_DECAY_RANK, D_RWKV), 0.1 * RWKV_DECAY_RANK ** -0.5),
        'rwkv_a0': nrm((L_, 2, D_RWKV), 0.1),
        'rwkv_a_up': nrm((L_, 2, RWKV_A_RANK, D_RWKV), 0.1 * RWKV_A_RANK ** -0.5),
        'rwkv_g_up': nrm((L_, RWKV_GATE_RANK, D_RWKV), RWKV_GATE_RANK ** -0.5),
        'rwkv_k_k': 0.85 + nrm((L_, D_RWKV), 0.02),
        'rwkv_k_a': 1.0 + nrm((L_, D_RWKV), 0.02),
        'rwkv_r_k': nrm((L_, RWKV_HEADS, RWKV_HEAD_DIM), 0.1),
        'rwkv_ln_g': 1.0 + nrm((L_, D_RWKV), 0.02),
        'rwkv_ln_b': nrm((L_, D_RWKV), 0.02),
        'ssd_conv_w': nrm((L_, CONV_WIDTH, SSD_CONV_DIM), CONV_WIDTH ** -0.5),
        'ssd_conv_b': nrm((L_, SSD_CONV_DIM), 0.02),
        'ssd_dt_bias': ssd_dt_bias,
        'ssd_a_log': jnp.log(unif((L_, 2, SSD_HEADS), 1.0, 16.0)),
        'ssd_d': 1.0 + nrm((L_, SSD_HEADS), 0.02),
        'ssd_norm_g': 1.0 + nrm((L_, D_SSD), 0.02),
        'w_out': nrm((L_, D_MIX, D_MODEL), D_MIX ** -0.5),
        'norm2_g': 1.0 + nrm((L_, D_MODEL), 0.02),
        'mlp_w1': nrm((L_, D_MODEL, D_FF), D_MODEL ** -0.5),
        'mlp_w2': nrm((L_, D_FF, D_MODEL), D_FF ** -0.5),
        'final_norm_g': 1.0 + nrm((D_MODEL,), 0.02),
    }


def reference(x_prompt, x_sample, norm1_g, w_in, lru_conv_w, lru_conv_b, lru_gate_w, lru_gate_b,
              lru_lambda, rwkv_mu, rwkv_w0, rwkv_w_up, rwkv_a0, rwkv_a_up, rwkv_g_up, rwkv_k_k,
              rwkv_k_a, rwkv_r_k, rwkv_ln_g, rwkv_ln_b, ssd_conv_w, ssd_conv_b, ssd_dt_bias,
              ssd_a_log, ssd_d, ssd_norm_g, w_out, norm2_g, mlp_w1, mlp_w2, final_norm_g):
    weights = (norm1_g, w_in, lru_conv_w, lru_conv_b, lru_gate_w, lru_gate_b, lru_lambda,
               rwkv_mu, rwkv_w0, rwkv_w_up, rwkv_a0, rwkv_a_up, rwkv_g_up, rwkv_k_k, rwkv_k_a,
               rwkv_r_k, rwkv_ln_g, rwkv_ln_b, ssd_conv_w, ssd_conv_b, ssd_dt_bias, ssd_a_log,
               ssd_d, ssd_norm_g, w_out, norm2_g, mlp_w1, mlp_w2, final_norm_g)
    y_prompt = encoder_trunk(x_prompt, weights)
    y_sample = encoder_trunk(x_sample, weights)
    return (y_prompt, y_sample)
```

```python
import functools
import math

import jax
import jax.numpy as jnp
from jax import lax
from jax.experimental import pallas as pl
from jax.experimental.pallas import tpu as pltpu

F32 = jnp.float32
BF16 = jnp.bfloat16

D_MODEL = 2048
D_FF = 4 * D_MODEL
NORM_EPS = 1e-6
HEAD_DIM = 64
D_LRU = 512
LRU_C = 8.0
D_RWKV = 768
RWKV_RANK = 96
RWKV_GATE_RANK = 256
RWKV_GN_EPS = 64e-5
RWKV_COLS = 3 * D_RWKV + 4 * RWKV_RANK + RWKV_GATE_RANK
RWKV_COLS_PAD = 3072
D_SSD = 768
SSD_HEADS = 12
SSD_GROUPS = 4
SSD_STATE = 128
SSD_BC = SSD_GROUPS * SSD_STATE
SSD_CONV_DIM = D_SSD + 2 * SSD_BC
SSD_COLS_PAD = D_SSD + SSD_CONV_DIM + 128
LRU_COLS = 2 * D_LRU

LANE = 128
SUBLANE = 8
VMEM_LIMIT = 56 * 1024 * 1024

TOK_BLOCK = 256
LRU_BLOCK = 512
SSD_CHUNK = 128
RWKV_CHUNK = 64
MM_TM = 512


def _bf(x):
    return x.astype(BF16)


def _mm(a, b):
    return jnp.dot(_bf(a), _bf(b), preferred_element_type=F32)


def _mm_nt(a, b):
    return lax.dot_general(_bf(a), _bf(b), (((1,), (1,)), ((), ())), preferred_element_type=F32)


def _mm_tn(a, b):
    return lax.dot_general(_bf(a), _bf(b), (((0,), (0,)), ((), ())), preferred_element_type=F32)


def _split3(x):
    x1 = _bf(x)
    r1 = x - x1.astype(F32)
    x2 = _bf(r1)
    x3 = _bf(r1 - x2.astype(F32))
    return x1, x2, x3


def _mm_sel_lhs(sel, x):
    x1, x2, x3 = _split3(x)
    d = lambda v: jnp.dot(sel, v, preferred_element_type=F32)
    return d(x1) + d(x2) + d(x3)


def _mm_sel_rhs(x, sel):
    x1, x2, x3 = _split3(x)
    d = lambda v: jnp.dot(v, sel, preferred_element_type=F32)
    return d(x1) + d(x2) + d(x3)


def _sigmoid(x):
    return 1.0 / (1.0 + jnp.exp(-x))


def _softplus(x):
    return jnp.maximum(x, 0.0) + jnp.log1p(jnp.exp(-jnp.abs(x)))


def _silu(x):
    return x * _sigmoid(x)


def _gelu_tanh(x):
    return 0.5 * x * (1.0 + jnp.tanh(math.sqrt(2.0 / math.pi) * (x + 0.044715 * (x * x * x))))


def _rms(x, g):
    return x * lax.rsqrt(jnp.mean(x * x, axis=-1, keepdims=True) + NORM_EPS) * g


def _block_pos(t0, segs):
    pos = None
    slen = None
    base = 0
    for n, length in segs:
        p = lax.rem(t0 - base, length)
        if pos is None:
            pos, slen = p, jnp.int32(length)
        else:
            inside = t0 >= base
            pos = jnp.where(inside, p, pos)
            slen = jnp.where(inside, jnp.int32(length), slen)
        base += n * length
    return pos, slen


def _total_tokens(segs):
    return sum(n * length for n, length in segs)


def _halo_specs(tb, width, nblocks_8, rev_nb=None):
    r = tb // SUBLANE
    if rev_nb is None:
        blk = lambda i: i
    else:
        blk = lambda i: rev_nb - 1 - i
    prev = pl.BlockSpec((SUBLANE, width), lambda i: (jnp.maximum(blk(i) * r - 1, 0), 0))
    nxt = pl.BlockSpec((SUBLANE, width), lambda i: (jnp.minimum((blk(i) + 1) * r, nblocks_8 - 1), 0))
    return prev, nxt


def _shifted(x, prev_row, next0, next1):
    n = x.shape[0]
    row = lax.broadcasted_iota(jnp.int32, (n, 1), 0)
    xm1 = jnp.where(row == 0, prev_row, pltpu.roll(x, 1, 0))
    xp1 = jnp.where(row == n - 1, next0, pltpu.roll(x, n - 1, 0))
    if next1 is None:
        return xm1, xp1, None
    xp2 = jnp.where(row == n - 2, next0, jnp.where(row == n - 1, next1, pltpu.roll(x, n - 2, 0)))
    return xm1, xp1, xp2


def _conv4(x, prev_row, next0, next1, w, b):
    xm1, xp1, xp2 = _shifted(x, prev_row, next0, next1)
    return w[0:1] * xm1 + w[1:2] * x + w[2:3] * xp1 + w[3:4] * xp2 + b


def _norm_matmul_kernel(x_ref, g_ref, w_ref, o_ref, u_ref):
    @pl.when(pl.program_id(1) == 0)
    def _():
        u_ref[...] = _bf(_rms(x_ref[...], g_ref[...]))

    o_ref[...] = jnp.dot(u_ref[...], w_ref[...], preferred_element_type=F32)


def _norm_matmul(x, g, w, tn):
    t, d = x.shape
    n = w.shape[1]
    tm = min(MM_TM, t)
    return pl.pallas_call(
        _norm_matmul_kernel,
        grid=(t // tm, n // tn),
        in_specs=[pl.BlockSpec((tm, d), lambda i, j: (i, 0)),
                  pl.BlockSpec((1, d), lambda i, j: (0, 0)),
                  pl.BlockSpec((d, tn), lambda i, j: (0, j))],
        out_specs=pl.BlockSpec((tm, tn), lambda i, j: (i, j)),
        out_shape=jax.ShapeDtypeStruct((t, n), F32),
        scratch_shapes=[pltpu.VMEM((tm, d), BF16)],
        compiler_params=pltpu.CompilerParams(
            dimension_semantics=("parallel", "arbitrary"), vmem_limit_bytes=VMEM_LIMIT),
        name="norm_matmul",
    )(x, g.reshape(1, d), w)


def _out_proj_kernel(x_ref, ya_ref, yb_ref, yc_ref, wa_ref, wb_ref, wc_ref, o_ref):
    acc = jnp.dot(_bf(ya_ref[...]), wa_ref[...], preferred_element_type=F32)
    acc += jnp.dot(_bf(yb_ref[...]), wb_ref[...], preferred_element_type=F32)
    acc += jnp.dot(_bf(yc_ref[...]), wc_ref[...], preferred_element_type=F32)
    o_ref[...] = x_ref[...] + acc


def _out_proj(x, y_lru, y_rwkv, y_ssd, wa, wb, wc):
    t, d = x.shape
    tm = min(MM_TM, t)
    tn = 1024
    return pl.pallas_call(
        _out_proj_kernel,
        grid=(t // tm, d // tn),
        in_specs=[pl.BlockSpec((tm, tn), lambda i, j: (i, j)),
                  pl.BlockSpec((tm, D_LRU), lambda i, j: (i, 0)),
                  pl.BlockSpec((tm, D_RWKV), lambda i, j: (i, 0)),
                  pl.BlockSpec((tm, D_SSD), lambda i, j: (i, 0)),
                  pl.BlockSpec((D_LRU, tn), lambda i, j: (0, j)),
                  pl.BlockSpec((D_RWKV, tn), lambda i, j: (0, j)),
                  pl.BlockSpec((D_SSD, tn), lambda i, j: (0, j))],
        out_specs=pl.BlockSpec((tm, tn), lambda i, j: (i, j)),
        out_shape=jax.ShapeDtypeStruct((t, d), F32),
        compiler_params=pltpu.CompilerParams(
            dimension_semantics=("parallel", "arbitrary"), vmem_limit_bytes=VMEM_LIMIT),
        name="out_proj",
    )(x, y_lru, y_rwkv, y_ssd, wa, wb, wc)


def _mlp_kernel(x_ref, g_ref, w1_ref, w2_ref, o_ref, u_ref):
    @pl.when(pl.program_id(1) == 0)
    def _():
        x = x_ref[...]
        u_ref[...] = _bf(_rms(x, g_ref[...]))
        o_ref[...] = x

    h = jnp.dot(u_ref[...], w1_ref[...], preferred_element_type=F32)
    h = jnp.square(jnp.maximum(h, 0.0))
    o_ref[...] += jnp.dot(_bf(h), w2_ref[...], preferred_element_type=F32)


def _mlp(x, g, w1, w2):
    t, d = x.shape
    ff = w1.shape[1]
    tm = min(MM_TM, t)
    tf = 512
    return pl.pallas_call(
        _mlp_kernel,
        grid=(t // tm, ff // tf),
        in_specs=[pl.BlockSpec((tm, d), lambda i, f: (i, 0)),
                  pl.BlockSpec((1, d), lambda i, f: (0, 0)),
                  pl.BlockSpec((d, tf), lambda i, f: (0, f)),
                  pl.BlockSpec((tf, d), lambda i, f: (f, 0))],
        out_specs=pl.BlockSpec((tm, d), lambda i, f: (i, 0)),
        out_shape=jax.ShapeDtypeStruct((t, d), F32),
        scratch_shapes=[pltpu.VMEM((tm, d), BF16)],
        compiler_params=pltpu.CompilerParams(
            dimension_semantics=("parallel", "arbitrary"), vmem_limit_bytes=VMEM_LIMIT),
        name="mlp",
    )(x, g.reshape(1, d), w1, w2)


def _final_norm_kernel(x_ref, g_ref, o_ref):
    o_ref[...] = _rms(x_ref[...], g_ref[...])


def _final_norm(x, g):
    t, d = x.shape
    tm = min(MM_TM, t)
    return pl.pallas_call(
        _final_norm_kernel,
        grid=(t // tm,),
        in_specs=[pl.BlockSpec((tm, d), lambda i: (i, 0)),
                  pl.BlockSpec((1, d), lambda i: (0, 0))],
        out_specs=pl.BlockSpec((tm, d), lambda i: (i, 0)),
        out_shape=jax.ShapeDtypeStruct((t, d), F32),
        compiler_params=pltpu.CompilerParams(dimension_semantics=("parallel",)),
        name="final_norm",
    )(x, g.reshape(1, d))


def _scan_rows(a, b, reverse):
    n = a.shape[0]
    row = lax.broadcasted_iota(jnp.int32, (n, 1), 0)
    s = 1
    while s < n:
        shift = n - s if reverse else s
        valid = (row < n - s) if reverse else (row >= s)
        a_sh = pltpu.roll(a, shift, 0)
        b_sh = pltpu.roll(b, shift, 0)
        b = jnp.where(valid, a * b_sh, 0.0) + b
        a = jnp.where(valid, a * a_sh, a)
        s *= 2
    return a, b


def _lru_fwd_kernel(segs, tb, p_ref, pprev_ref, pnext_ref, cw_ref, cb_ref, wbd_ref, gb_ref, lam_ref,
                    hf_ref, a1_ref, bx1_ref, carry_ref):
    i = pl.program_id(0)
    pos, slen = _block_pos(i * tb, segs)
    is_start = pos == 0
    is_end = pos + tb == slen
    xb = p_ref[:, D_LRU:]
    prev_row = jnp.where(is_start, 0.0, pprev_ref[SUBLANE - 1:SUBLANE, D_LRU:])
    next0 = jnp.where(is_end, 0.0, pnext_ref[0:1, D_LRU:])
    next1 = jnp.where(is_end, 0.0, pnext_ref[1:2, D_LRU:])
    xc = _conv4(xb, prev_row, next0, next1, cw_ref[...], cb_ref[...])
    gates = _sigmoid(_mm(xc, wbd_ref[...]) + gb_ref[...])
    sp = _softplus(-lam_ref[...])

    def direction(d):
        r = gates[:, (2 * d) * D_LRU:(2 * d + 1) * D_LRU]
        inp = gates[:, (2 * d + 1) * D_LRU:(2 * d + 2) * D_LRU]
        log_a = -LRU_C * r * sp[d:d + 1]
        th = jnp.tanh(log_a)
        one_minus_a2 = -2.0 * th / (1.0 - th)
        return jnp.exp(log_a), jnp.sqrt(one_minus_a2) * (inp * xc)

    a0, bx0 = direction(0)
    a1, bx1 = direction(1)
    a1_ref[...] = a1
    bx1_ref[...] = bx1

    @pl.when(is_start)
    def _():
        carry_ref[...] = jnp.zeros_like(carry_ref)

    a_cum, h_loc = _scan_rows(a0, bx0, False)
    h = h_loc + a_cum * carry_ref[...]
    hf_ref[...] = h
    carry_ref[...] = h[tb - 1:tb]


def _lru_bwd_kernel(segs, tb, nb, p_ref, a1_ref, bx1_ref, hf_ref, o_ref, carry_ref):
    blk = nb - 1 - pl.program_id(0)
    pos, slen = _block_pos(blk * tb, segs)

    @pl.when(pos + tb == slen)
    def _():
        carry_ref[...] = jnp.zeros_like(carry_ref)

    a_cum, h_loc = _scan_rows(a1_ref[...], bx1_ref[...], True)
    h = h_loc + a_cum * carry_ref[...]
    carry_ref[...] = h[0:1]
    o_ref[...] = (hf_ref[...] + h) * _gelu_tanh(p_ref[:, :D_LRU])


def _lru_mixer(p, segs, conv_w, conv_b, w_bd, gate_b, lam):
    t = p.shape[0]
    tb = min(LRU_BLOCK, min(length for _, length in segs))
    nb = t // tb
    prev_spec, next_spec = _halo_specs(tb, LRU_COLS, t // SUBLANE)
    const = lambda shape: pl.BlockSpec(shape, lambda i: (0,) * len(shape))
    row_spec = lambda w: pl.BlockSpec((tb, w), lambda i: (i, 0))
    hf, a1, bx1 = pl.pallas_call(
        functools.partial(_lru_fwd_kernel, segs, tb),
        grid=(nb,),
        in_specs=[row_spec(LRU_COLS), prev_spec, next_spec, const((4, D_LRU)), const((1, D_LRU)),
                  const((D_LRU, 4 * D_LRU)), const((1, 4 * D_LRU)), const((2, D_LRU))],
        out_specs=[row_spec(D_LRU)] * 3,
        out_shape=[jax.ShapeDtypeStruct((t, D_LRU), F32)] * 3,
        scratch_shapes=[pltpu.VMEM((1, D_LRU), F32)],
        compiler_params=pltpu.CompilerParams(
            dimension_semantics=("arbitrary",), vmem_limit_bytes=VMEM_LIMIT),
        name="lru_fwd",
    )(p, p, p, conv_w, conv_b.reshape(1, D_LRU), w_bd, gate_b.reshape(1, 4 * D_LRU), lam)
    rev_spec = lambda w: pl.BlockSpec((tb, w), lambda i: (nb - 1 - i, 0))
    return pl.pallas_call(
        functools.partial(_lru_bwd_kernel, segs, tb, nb),
        grid=(nb,),
        in_specs=[rev_spec(LRU_COLS), rev_spec(D_LRU), rev_spec(D_LRU), rev_spec(D_LRU)],
        out_specs=rev_spec(D_LRU),
        out_shape=jax.ShapeDtypeStruct((t, D_LRU), F32),
        scratch_shapes=[pltpu.VMEM((1, D_LRU), F32)],
        compiler_params=pltpu.CompilerParams(
            dimension_semantics=("arbitrary",), vmem_limit_bytes=VMEM_LIMIT),
        name="lru_bwd",
    )(p, a1, bx1, hf)


def _ssd_core(reverse, xs, bm, cm, dt, da, ht_ref):
    q = xs.shape[0]
    off = SSD_HEADS if reverse else 0
    row = lax.broadcasted_iota(jnp.int32, (q, q), 0)
    col = lax.broadcasted_iota(jnp.int32, (q, q), 1)
    lower = _bf((col <= row).astype(F32))
    cum = _mm_sel_lhs(lower, da)
    total = cum[q - 1:q]
    if reverse:
        u = cum - da
        mask = col >= row
        out_fac = jnp.exp(total - u)
        in_fac = jnp.exp(u)
    else:
        u = cum
        mask = col <= row
        out_fac = jnp.exp(u)
        in_fac = jnp.exp(total - u)
    chunk_decay = jnp.exp(total)
    u_t = u.T
    lane = lax.broadcasted_iota(jnp.int32, (1, LANE), 1)
    first = lane < HEAD_DIM
    cb = []
    for g in range(SSD_GROUPS):
        sl = slice(g * SSD_STATE, (g + 1) * SSD_STATE)
        cb.append(_mm_nt(cm[:, sl], bm[:, sl]))
    ys = []
    for pr in range(SSD_HEADS // 2):
        x_pair = xs[:, pr * LANE:(pr + 1) * LANE]
        ht = ht_ref[pr]
        y_e = []
        h_e = []
        for e in range(2):
            h = 2 * pr + e
            g = h // (SSD_HEADS // SSD_GROUPS)
            c = off + h
            sl = slice(g * SSD_STATE, (g + 1) * SSD_STATE)
            u_col = u[:, c:c + 1]
            u_row = u_t[c:c + 1, :]
            seg = (u_row - u_col) if reverse else (u_col - u_row)
            decay = jnp.exp(jnp.where(mask, seg, -1e30))
            xd = x_pair * dt[:, c:c + 1]
            y = _mm(cb[g] * decay, xd) + _mm(cm[:, sl], ht) * out_fac[:, c:c + 1]
            y_e.append(y)
            h_e.append(chunk_decay[:, c:c + 1] * ht + _mm_tn(bm[:, sl], xd * in_fac[:, c:c + 1]))
        ys.append(jnp.where(first, y_e[0], y_e[1]))
        ht_ref[pr] = jnp.where(first, h_e[0], h_e[1])
    return ys


def _ssd_dt(p_ref, dtb_ref, alog_ref):
    dt = _softplus(p_ref[:, D_SSD + SSD_CONV_DIM:] + dtb_ref[...])
    lane = lax.broadcasted_iota(jnp.int32, (1, LANE), 1)
    a_head = jnp.where(lane < 2 * SSD_HEADS, -jnp.exp(alog_ref[...]), 0.0)
    return dt, dt * a_head


def _ssd_fwd_kernel(segs, q, p_ref, pprev_ref, pnext_ref, cw_ref, cb_ref, dtb_ref, alog_ref, dskip_ref,
                    xconv_ref, yacc_ref, ht_ref):
    i = pl.program_id(0)
    pos, slen = _block_pos(i * q, segs)
    is_start = pos == 0
    is_end = pos + q == slen
    lo, hi = D_SSD, D_SSD + SSD_CONV_DIM
    prev_row = jnp.where(is_start, 0.0, pprev_ref[SUBLANE - 1:SUBLANE, lo:hi])
    next0 = jnp.where(is_end, 0.0, pnext_ref[0:1, lo:hi])
    next1 = jnp.where(is_end, 0.0, pnext_ref[1:2, lo:hi])
    xbc = _silu(_conv4(p_ref[:, lo:hi], prev_row, next0, next1, cw_ref[...], cb_ref[...]))
    xconv_ref[...] = xbc
    xs = xbc[:, :D_SSD]
    dt, da = _ssd_dt(p_ref, dtb_ref, alog_ref)

    @pl.when(is_start)
    def _():
        ht_ref[...] = jnp.zeros_like(ht_ref)

    ys = _ssd_core(False, xs, xbc[:, D_SSD:D_SSD + SSD_BC], xbc[:, D_SSD + SSD_BC:], dt, da, ht_ref)
    for pr, y in enumerate(ys):
        sl = slice(pr * LANE, (pr + 1) * LANE)
        yacc_ref[:, sl] = y + dskip_ref[:, sl] * xs[:, sl]


def _ssd_bwd_kernel(segs, q, nb, p_ref, xconv_ref, yacc_ref, dtb_ref, alog_ref, ng_ref, o_ref, ht_ref):
    blk = nb - 1 - pl.program_id(0)
    pos, slen = _block_pos(blk * q, segs)

    @pl.when(pos + q == slen)
    def _():
        ht_ref[...] = jnp.zeros_like(ht_ref)

    xbc = xconv_ref[...]
    dt, da = _ssd_dt(p_ref, dtb_ref, alog_ref)
    ys = _ssd_core(True, xbc[:, :D_SSD], xbc[:, D_SSD:D_SSD + SSD_BC], xbc[:, D_SSD + SSD_BC:], dt, da, ht_ref)
    y = jnp.concatenate(ys, axis=1) + yacc_ref[...]
    o_ref[...] = _rms(y * _silu(p_ref[:, :D_SSD]), ng_ref[...])


def _ssd_mixer(p, segs, conv_w, conv_b, dt_bias, a_log, d_skip, norm_g):
    t = p.shape[0]
    q = SSD_CHUNK
    nb = t // q
    prev_spec, next_spec = _halo_specs(q, SSD_COLS_PAD, t // SUBLANE)
    const = lambda shape: pl.BlockSpec(shape, lambda i: (0,) * len(shape))
    row_spec = lambda w: pl.BlockSpec((q, w), lambda i: (i, 0))
    state = pltpu.VMEM((SSD_HEADS // 2, SSD_STATE, LANE), F32)
    xconv, yacc = pl.pallas_call(
        functools.partial(_ssd_fwd_kernel, segs, q),
        grid=(nb,),
        in_specs=[row_spec(SSD_COLS_PAD), prev_spec, next_spec, const((4, SSD_CONV_DIM)),
                  const((1, SSD_CONV_DIM)), const((1, LANE)), const((1, LANE)), const((1, D_SSD))],
        out_specs=[row_spec(SSD_CONV_DIM), row_spec(D_SSD)],
        out_shape=[jax.ShapeDtypeStruct((t, SSD_CONV_DIM), F32), jax.ShapeDtypeStruct((t, D_SSD), F32)],
        scratch_shapes=[state],
        compiler_params=pltpu.CompilerParams(
            dimension_semantics=("arbitrary",), vmem_limit_bytes=VMEM_LIMIT),
        name="ssd_fwd",
    )(p, p, p, conv_w, conv_b.reshape(1, SSD_CONV_DIM), dt_bias, a_log, d_skip)
    rev_spec = lambda w: pl.BlockSpec((q, w), lambda i: (nb - 1 - i, 0))
    return pl.pallas_call(
        functools.partial(_ssd_bwd_kernel, segs, q, nb),
        grid=(nb,),
        in_specs=[rev_spec(SSD_COLS_PAD), rev_spec(SSD_CONV_DIM), rev_spec(D_SSD),
                  const((1, LANE)), const((1, LANE)), const((1, D_SSD))],
        out_specs=rev_spec(D_SSD),
        out_shape=jax.ShapeDtypeStruct((t, D_SSD), F32),
        scratch_shapes=[state],
        compiler_params=pltpu.CompilerParams(
            dimension_semantics=("arbitrary",), vmem_limit_bytes=VMEM_LIMIT),
        name="ssd_bwd",
    )(p, xconv, yacc, dt_bias, a_log, norm_g.reshape(1, D_SSD))


def _rwkv_pre_kernel(segs, tb, p_ref, pprev_ref, pnext_ref, mu_ref, wwa_ref, w0a0_ref, gup_ref, kk_ref_w,
                     ka_ref, rk_ref, hsum_ref,
                     r_out, v_out, kk_out, lw0_out, lw1_out, kd0_out, kd1_out, b0_out, b1_out, g_out, bon_out):
    i = pl.program_id(0)
    pos, slen = _block_pos(i * tb, segs)
    p = p_ref[...]
    prev_row = jnp.where(pos == 0, 0.0, pprev_ref[SUBLANE - 1:SUBLANE, :])
    next_row = jnp.where(pos + tb == slen, 0.0, pnext_ref[0:1, :])
    pm1, pp1, _ = _shifted(p, prev_row, next_row, None)
    mu = mu_ref[...]
    ps = p + mu[0:1] * (pm1 - p) + mu[1:2] * (pp1 - p)
    c = D_RWKV
    r = ps[:, 0:c]
    k = ps[:, c:2 * c]
    v = ps[:, 2 * c:3 * c]
    lo = ps[:, 3 * c:3 * c + 4 * RWKV_RANK]
    g_lo = ps[:, 3 * c + 4 * RWKV_RANK:3 * c + 4 * RWKV_RANK + RWKV_GATE_RANK]
    lane = lax.broadcasted_iota(jnp.int32, (1, 4 * RWKV_RANK), 1)
    lo = jnp.where(lane < 2 * RWKV_RANK, jnp.tanh(lo), lo)
    pre = _mm(lo, wwa_ref[...]) + w0a0_ref[...]
    g = _mm(_sigmoid(g_lo), gup_ref[...])
    hsum = hsum_ref[...]
    kk = k * kk_ref_w[...]
    ss = _mm_sel_rhs(kk * kk, hsum)
    kk = kk / jnp.maximum(jnp.sqrt(ss), 1e-12)
    kd_sum = None
    outs = ((lw0_out, kd0_out, b0_out), (lw1_out, kd1_out, b1_out))
    for d in range(2):
        w_pre = pre[:, d * c:(d + 1) * c]
        a = _sigmoid(pre[:, (2 + d) * c:(3 + d) * c])
        outs[d][0][...] = -math.exp(-0.5) * _sigmoid(w_pre)
        kd = k * (1.0 + (a - 1.0) * ka_ref[...])
        outs[d][1][...] = kd
        outs[d][2][...] = kk * a
        kd_sum = kd if kd_sum is None else kd_sum + kd
    r_out[...] = r
    v_out[...] = v
    kk_out[...] = kk
    g_out[...] = g
    bon_out[...] = _mm_sel_rhs(r * kd_sum * rk_ref[...], hsum) * v


def _rwkv_scan_kernel(reverse, segs, cl, nb, r_ref, v_ref, kk_ref, lw_ref, kd_ref, b_ref, y_ref, st_ref):
    i = pl.program_id(0)
    blk = nb - 1 - i if reverse else i
    pos, slen = _block_pos(blk * cl, segs)
    boundary = (pos + cl == slen) if reverse else (pos == 0)

    @pl.when(boundary)
    def _():
        st_ref[...] = jnp.zeros_like(st_ref)

    n2 = 2 * cl
    rowc = lax.broadcasted_iota(jnp.int32, (cl, cl), 0)
    colc = lax.broadcasted_iota(jnp.int32, (cl, cl), 1)
    tri = _bf(((colc >= rowc) if reverse else (colc <= rowc)).astype(F32))
    lw = lw_ref[...]
    cum = _mm_sel_lhs(tri, lw)
    total = cum[0:1] if reverse else cum[cl - 1:cl]
    g_in = jnp.exp(cum)
    g_prev = jnp.exp(cum - lw)
    g_inv = jnp.exp(-cum)
    g_end = jnp.exp(total - cum)
    g_tot = jnp.exp(total)
    kk = kk_ref[...]
    zt = -kk * g_prev
    rt = r_ref[...] * g_in
    bh = b_ref[...] * g_inv
    kh = kd_ref[...] * g_inv
    bg = b_ref[...] * g_end
    kg = kd_ref[...] * g_end
    v = v_ref[...]

    row = lax.broadcasted_iota(jnp.int32, (n2, n2), 0)
    col = lax.broadcasted_iota(jnp.int32, (n2, n2), 1)
    same = (row >= cl) == (col >= cl)
    rt_i = jnp.where(row >= cl, row - cl, row)
    ct_i = jnp.where(col >= cl, col - cl, col)
    if reverse:
        m_strict = same & (ct_i > rt_i)
        m_incl = same & (ct_i >= rt_i)
    else:
        m_strict = same & (ct_i < rt_i)
        m_incl = same & (ct_i <= rt_i)
    eye = row == col
    first = lax.broadcasted_iota(jnp.int32, (1, LANE), 1) < HEAD_DIM
    pair_mask = same & ((rt_i >> 1) == (ct_i >> 1))
    join_masks = []
    shift = 1
    while (1 << shift) < cl:
        later, earlier = (ct_i, rt_i) if reverse else (rt_i, ct_i)
        join_masks.append(same & ((rt_i >> (shift + 1)) == (ct_i >> (shift + 1)))
                          & (((later >> shift) & 1) == 1) & (((earlier >> shift) & 1) == 0))
        shift += 1

    def stack(x):
        return jnp.concatenate([jnp.where(first, x, 0.0), jnp.where(first, 0.0, x)], axis=0)

    for pr in range(D_RWKV // LANE):
        sl = slice(pr * LANE, (pr + 1) * LANE)
        ztm, rtm, bhm, khm, bgm, kgm, vm = (stack(x[:, sl]) for x in (zt, rt, bh, kh, bg, kg, v))
        zr = jnp.concatenate([ztm, rtm], axis=0)
        gb = _mm_nt(zr, bhm)
        gk = _mm_nt(zr, khm)
        a_ab = jnp.where(m_strict, gb[:n2], 0.0)
        a_ak = jnp.where(m_strict, gk[:n2], 0.0)
        a_rb = jnp.where(m_incl, gb[n2:], 0.0)
        a_rk = jnp.where(m_incl, gk[n2:], 0.0)
        tinv = jnp.where(eye, 1.0, jnp.where(pair_mask, a_ab, 0.0))
        for jm in join_masks:
            tinv = tinv + _mm(_mm(tinv, jnp.where(jm, a_ab, 0.0)), tinv)
        x = _mm(tinv, jnp.concatenate([ztm, _mm(a_ak, vm)], axis=1))
        w = _mm(a_rb, x)
        r_eff = rtm + w[:, :LANE]
        y_loc = w[:, LANE:] + _mm(a_rk, vm)
        st = st_ref[pr]
        ym = y_loc + _mm(r_eff, st)
        p_mat = jnp.where(eye, g_tot[:, sl], 0.0) + _mm_tn(bgm, x[:, :LANE])
        q_mat = _mm_tn(bgm, x[:, LANE:]) + _mm_tn(kgm, vm)
        st_ref[pr] = _mm(p_mat, st) + q_mat
        y_ref[:, sl] = ym[:cl] + ym[cl:]


def _rwkv_post_kernel(yf_ref, yb_ref, bon_ref, g_ref, lng_ref, lnb_ref, hsum_ref, o_ref):
    y = yf_ref[...] + yb_ref[...]
    hsum = hsum_ref[...]
    mean = _mm_sel_rhs(y, hsum) * (1.0 / HEAD_DIM)
    d = y - mean
    var = _mm_sel_rhs(d * d, hsum) * (1.0 / HEAD_DIM)
    y = d * lax.rsqrt(var + RWKV_GN_EPS) * lng_ref[...] + lnb_ref[...]
    o_ref[...] = (y + bon_ref[...]) * g_ref[...]


def _rwkv_mixer(p, segs, mu, w_wa, w0a0, g_up, k_k, k_a, r_k, ln_g, ln_b, hsum):
    t = p.shape[0]
    c = D_RWKV
    tb = min(TOK_BLOCK, min(length for _, length in segs))
    prev_spec, next_spec = _halo_specs(tb, RWKV_COLS_PAD, t // SUBLANE)
    const = lambda shape: pl.BlockSpec(shape, lambda i: (0,) * len(shape))
    row_spec = lambda n, w: pl.BlockSpec((n, w), lambda i: (i, 0))
    vec = lambda a: a.reshape(1, c)
    tok = jax.ShapeDtypeStruct((t, c), F32)
    r, v, kk, lw0, lw1, kd0, kd1, b0, b1, g, bon = pl.pallas_call(
        functools.partial(_rwkv_pre_kernel, segs, tb),
        grid=(t // tb,),
        in_specs=[row_spec(tb, RWKV_COLS_PAD), prev_spec, next_spec, const((2, RWKV_COLS_PAD)),
                  const((4 * RWKV_RANK, 4 * c)), const((1, 4 * c)), const((RWKV_GATE_RANK, c)),
                  const((1, c)), const((1, c)), const((1, c)), const((c, c))],
        out_specs=[row_spec(tb, c)] * 11,
        out_shape=[tok] * 11,
        compiler_params=pltpu.CompilerParams(
            dimension_semantics=("parallel",), vmem_limit_bytes=VMEM_LIMIT),
        name="rwkv_pre",
    )(p, p, p, mu, w_wa, w0a0, g_up, vec(k_k), vec(k_a), vec(r_k), hsum)

    cl = RWKV_CHUNK
    nb = t // cl
    ys = []
    for reverse, lw, kd, b in ((False, lw0, kd0, b0), (True, lw1, kd1, b1)):
        spec = pl.BlockSpec((cl, c), (lambda i: (nb - 1 - i, 0)) if reverse else (lambda i: (i, 0)))
        ys.append(pl.pallas_call(
            functools.partial(_rwkv_scan_kernel, reverse, segs, cl, nb),
            grid=(nb,),
            in_specs=[spec] * 6,
            out_specs=spec,
            out_shape=tok,
            scratch_shapes=[pltpu.VMEM((c // LANE, LANE, LANE), F32)],
            compiler_params=pltpu.CompilerParams(
                dimension_semantics=("arbitrary",), vmem_limit_bytes=VMEM_LIMIT),
            name="rwkv_scan_bwd" if reverse else "rwkv_scan_fwd",
        )(r, v, kk, lw, kd, b))

    return pl.pallas_call(
        _rwkv_post_kernel,
        grid=(t // tb,),
        in_specs=[row_spec(tb, c)] * 4 + [const((1, c)), const((1, c)), const((c, c))],
        out_specs=row_spec(tb, c),
        out_shape=tok,
        compiler_params=pltpu.CompilerParams(
            dimension_semantics=("parallel",), vmem_limit_bytes=VMEM_LIMIT),
        name="rwkv_post",
    )(ys[0], ys[1], bon, g, vec(ln_g), vec(ln_b), hsum)


def _pad_cols(w, n):
    return jnp.pad(w, ((0, 0), (0, n - w.shape[1])))


def _lru_gate_blockdiag(gate_w):
    nh = gate_w.shape[2]
    eye = jnp.eye(nh, dtype=gate_w.dtype)
    w = jnp.einsum('dghij,hk->hidgkj', gate_w, eye)
    return w.reshape(D_LRU, 4 * D_LRU)


def _rwkv_lowrank_blocks(w_up, a_up):
    blocks = [w_up[0], w_up[1], a_up[0], a_up[1]]
    out = jnp.zeros((4 * RWKV_RANK, 4 * D_RWKV), w_up.dtype)
    for n, blk in enumerate(blocks):
        out = lax.dynamic_update_slice(out, blk, (n * RWKV_RANK, n * D_RWKV))
    return out


def _encoder(x, segs, w):
    (norm1_g, w_in, lru_conv_w, lru_conv_b, lru_gate_w, lru_gate_b, lru_lambda,
     rwkv_mu, rwkv_w0, rwkv_w_up, rwkv_a0, rwkv_a_up, rwkv_g_up, rwkv_k_k, rwkv_k_a,
     rwkv_r_k, rwkv_ln_g, rwkv_ln_b, ssd_conv_w, ssd_conv_b, ssd_dt_bias, ssd_a_log,
     ssd_d, ssd_norm_g, w_out, norm2_g, mlp_w1, mlp_w2, final_norm_g) = w
    depth = w_in.shape[0]
    head_id = jnp.arange(D_RWKV) // HEAD_DIM
    hsum = (head_id[:, None] == head_id[None, :]).astype(BF16)
    o1 = LRU_COLS
    o2 = LRU_COLS + RWKV_COLS
    for l in range(depth):
        w_in_l = w_in[l].astype(BF16)
        p_lru = _norm_matmul(x, norm1_g[l], w_in_l[:, :o1], 512)
        p_rwkv = _norm_matmul(x, norm1_g[l], _pad_cols(w_in_l[:, o1:o2], RWKV_COLS_PAD), 512)
        p_ssd = _norm_matmul(x, norm1_g[l], _pad_cols(w_in_l[:, o2:], SSD_COLS_PAD), 384)

        y_lru = _lru_mixer(p_lru, segs, lru_conv_w[l], lru_conv_b[l],
                           _lru_gate_blockdiag(lru_gate_w[l]).astype(BF16), lru_gate_b[l], lru_lambda[l])

        y_rwkv = _rwkv_mixer(
            p_rwkv, segs, _pad_cols(rwkv_mu[l], RWKV_COLS_PAD),
            _rwkv_lowrank_blocks(rwkv_w_up[l], rwkv_a_up[l]).astype(BF16),
            jnp.concatenate([rwkv_w0[l, 0], rwkv_w0[l, 1], rwkv_a0[l, 0], rwkv_a0[l, 1]]).reshape(1, 4 * D_RWKV),
            rwkv_g_up[l].astype(BF16), rwkv_k_k[l], rwkv_k_a[l], rwkv_r_k[l], rwkv_ln_g[l], rwkv_ln_b[l], hsum)

        y_ssd = _ssd_mixer(
            p_ssd, segs, ssd_conv_w[l], ssd_conv_b[l],
            _pad_cols(ssd_dt_bias[l].reshape(1, 2 * SSD_HEADS), LANE),
            _pad_cols(ssd_a_log[l].reshape(1, 2 * SSD_HEADS), LANE),
            jnp.repeat(ssd_d[l], HEAD_DIM).reshape(1, D_SSD), ssd_norm_g[l])

        w_out_l = w_out[l].astype(BF16)
        x = _out_proj(x, y_lru, y_rwkv, y_ssd, w_out_l[:D_LRU], w_out_l[D_LRU:D_LRU + D_RWKV],
                      w_out_l[D_LRU + D_RWKV:])
        x = _mlp(x, norm2_g[l], mlp_w1[l].astype(BF16), mlp_w2[l].astype(BF16))
    return _final_norm(x, final_norm_g)


def kernel(x_prompt, x_sample, norm1_g, w_in, lru_conv_w, lru_conv_b, lru_gate_w, lru_gate_b, lru_lambda, rwkv_mu, rwkv_w0, rwkv_w_up, rwkv_a0, rwkv_a_up, rwkv_g_up, rwkv_k_k, rwkv_k_a, rwkv_r_k, rwkv_ln_g, rwkv_ln_b, ssd_conv_w, ssd_conv_b, ssd_dt_bias, ssd_a_log, ssd_d, ssd_norm_g, w_out, norm2_g, mlp_w1, mlp_w2, final_norm_g):
    weights = (norm1_g, w_in, lru_conv_w, lru_conv_b, lru_gate_w, lru_gate_b, lru_lambda,
               rwkv_mu, rwkv_w0, rwkv_w_up, rwkv_a0, rwkv_a_up, rwkv_g_up, rwkv_k_k, rwkv_k_a,
               rwkv_r_k, rwkv_ln_g, rwkv_ln_b, ssd_conv_w, ssd_conv_b, ssd_dt_bias, ssd_a_log,
               ssd_d, ssd_norm_g, w_out, norm2_g, mlp_w1, mlp_w2, final_norm_g)
    bp, lp, d = x_prompt.shape
    bs, ls, _ = x_sample.shape
    segs = ((bp, lp), (bs, ls))
    x = jnp.concatenate([x_prompt.reshape(bp * lp, d), x_sample.reshape(bs * ls, d)], axis=0)
    y = _encoder(x, segs, weights)
    return y[:bp * lp].reshape(bp, lp, d), y[bp * lp:].reshape(bs, ls, d)
```

```python
import functools
import math

import jax
import jax.numpy as jnp
from jax import lax
from jax.experimental import pallas as pl
from jax.experimental.pallas import tpu as pltpu

F32 = jnp.float32
BF16 = jnp.bfloat16

D_MODEL = 2048
D_FF = 4 * D_MODEL
NORM_EPS = 1e-6
HEAD_DIM = 64
D_LRU = 512
LRU_C = 8.0
D_RWKV = 768
RWKV_RANK = 96
RWKV_GATE_RANK = 256
RWKV_LO_WIN = 256
RWKV_GN_EPS = 64e-5
RWKV_COLS = 3 * D_RWKV + 4 * RWKV_RANK + RWKV_GATE_RANK
RWKV_COLS_PAD = 3072
D_SSD = 768
SSD_HEADS = 12
SSD_GROUPS = 4
SSD_STATE = 128
SSD_BC = SSD_GROUPS * SSD_STATE
SSD_CONV_DIM = D_SSD + 2 * SSD_BC
SSD_COLS_PAD = D_SSD + SSD_CONV_DIM + 128
LRU_COLS = 2 * D_LRU

LANE = 128
SUBLANE = 8
VMEM_LIMIT = 56 * 1024 * 1024

TOK_BLOCK = 256
LRU_BLOCK = 512
SSD_CHUNK = 128
RWKV_CHUNK = 64
MM_TM = 1024
FINAL_TM = 512


def _bf(x):
    return x.astype(BF16)


def _mm(a, b):
    return jnp.dot(_bf(a), _bf(b), preferred_element_type=F32)


def _mm_nt(a, b):
    return lax.dot_general(_bf(a), _bf(b), (((1,), (1,)), ((), ())), preferred_element_type=F32)


def _mm_tn(a, b):
    return lax.dot_general(_bf(a), _bf(b), (((0,), (0,)), ((), ())), preferred_element_type=F32)


def _split3(x):
    x1 = _bf(x)
    r1 = x - x1.astype(F32)
    x2 = _bf(r1)
    x3 = _bf(r1 - x2.astype(F32))
    return x1, x2, x3


def _mm_sel_lhs(sel, x):
    x1, x2, x3 = _split3(x)
    d = lambda v: jnp.dot(sel, v, preferred_element_type=F32)
    return d(x1) + d(x2) + d(x3)


def _mm_sel_rhs2(x, sel):
    x1 = _bf(x)
    x2 = _bf(x - x1.astype(F32))
    return jnp.dot(x1, sel, preferred_element_type=F32) + jnp.dot(x2, sel, preferred_element_type=F32)


def _head_sum(x, sel, sel_t):
    return _mm_sel_rhs2(_mm_sel_rhs2(x, sel), sel_t)


def _sigmoid(x):
    return 1.0 / (1.0 + jnp.exp(-x))


def _softplus(x):
    return jnp.maximum(x, 0.0) + jnp.log1p(jnp.exp(-jnp.abs(x)))


def _silu(x):
    return x * _sigmoid(x)


def _gelu_tanh(x):
    return 0.5 * x * (1.0 + jnp.tanh(math.sqrt(2.0 / math.pi) * (x + 0.044715 * (x * x * x))))


def _rms(x, g):
    return x * lax.rsqrt(jnp.mean(x * x, axis=-1, keepdims=True) + NORM_EPS) * g


def _block_pos(t0, segs):
    pos = None
    slen = None
    base = 0
    for n, length in segs:
        p = lax.rem(t0 - base, length)
        if pos is None:
            pos, slen = p, jnp.int32(length)
        else:
            inside = t0 >= base
            pos = jnp.where(inside, p, pos)
            slen = jnp.where(inside, jnp.int32(length), slen)
        base += n * length
    return pos, slen


def _total_tokens(segs):
    return sum(n * length for n, length in segs)


def _halo_specs(tb, width, nblocks_8, rev_nb=None):
    r = tb // SUBLANE
    if rev_nb is None:
        blk = lambda i: i
    else:
        blk = lambda i: rev_nb - 1 - i
    prev = pl.BlockSpec((SUBLANE, width), lambda i: (jnp.maximum(blk(i) * r - 1, 0), 0))
    nxt = pl.BlockSpec((SUBLANE, width), lambda i: (jnp.minimum((blk(i) + 1) * r, nblocks_8 - 1), 0))
    return prev, nxt


def _shifted(x, prev_row, next0, next1):
    n = x.shape[0]
    row = lax.broadcasted_iota(jnp.int32, (n, 1), 0)
    xm1 = jnp.where(row == 0, prev_row, pltpu.roll(x, 1, 0))
    xp1 = jnp.where(row == n - 1, next0, pltpu.roll(x, n - 1, 0))
    if next1 is None:
        return xm1, xp1, None
    xp2 = jnp.where(row == n - 2, next0, jnp.where(row == n - 1, next1, pltpu.roll(x, n - 2, 0)))
    return xm1, xp1, xp2


def _conv4(x, prev_row, next0, next1, w, b):
    xm1, xp1, xp2 = _shifted(x, prev_row, next0, next1)
    return w[0:1] * xm1 + w[1:2] * x + w[2:3] * xp1 + w[3:4] * xp2 + b


def _norm_matmul_kernel(x_ref, g_ref, w_ref, o_ref, u_ref):
    @pl.when(pl.program_id(1) == 0)
    def _():
        u_ref[...] = _bf(_rms(x_ref[...], g_ref[...]))

    o_ref[...] = jnp.dot(u_ref[...], w_ref[...], preferred_element_type=F32)


def _norm_matmul(x, g, w, tn):
    t, d = x.shape
    n = w.shape[1]
    tm = min(MM_TM, t)
    return pl.pallas_call(
        _norm_matmul_kernel,
        grid=(t // tm, n // tn),
        in_specs=[pl.BlockSpec((tm, d), lambda i, j: (i, 0)),
                  pl.BlockSpec((1, d), lambda i, j: (0, 0)),
                  pl.BlockSpec((d, tn), lambda i, j: (0, j))],
        out_specs=pl.BlockSpec((tm, tn), lambda i, j: (i, j)),
        out_shape=jax.ShapeDtypeStruct((t, n), F32),
        scratch_shapes=[pltpu.VMEM((tm, d), BF16)],
        compiler_params=pltpu.CompilerParams(
            dimension_semantics=("parallel", "arbitrary"), vmem_limit_bytes=VMEM_LIMIT),
        name="norm_matmul",
    )(x, g.reshape(1, d), w)


def _out_proj_kernel(x_ref, ya_ref, yb_ref, yc_ref, wa_ref, wb_ref, wc_ref, o_ref):
    acc = jnp.dot(_bf(ya_ref[...]), wa_ref[...], preferred_element_type=F32)
    acc += jnp.dot(_bf(yb_ref[...]), wb_ref[...], preferred_element_type=F32)
    acc += jnp.dot(_bf(yc_ref[...]), wc_ref[...], preferred_element_type=F32)
    o_ref[...] = x_ref[...] + acc


def _out_proj(x, y_lru, y_rwkv, y_ssd, wa, wb, wc):
    t, d = x.shape
    tm = min(MM_TM, t)
    tn = 1024
    return pl.pallas_call(
        _out_proj_kernel,
        grid=(t // tm, d // tn),
        in_specs=[pl.BlockSpec((tm, tn), lambda i, j: (i, j)),
                  pl.BlockSpec((tm, D_LRU), lambda i, j: (i, 0)),
                  pl.BlockSpec((tm, D_RWKV), lambda i, j: (i, 0)),
                  pl.BlockSpec((tm, D_SSD), lambda i, j: (i, 0)),
                  pl.BlockSpec((D_LRU, tn), lambda i, j: (0, j)),
                  pl.BlockSpec((D_RWKV, tn), lambda i, j: (0, j)),
                  pl.BlockSpec((D_SSD, tn), lambda i, j: (0, j))],
        out_specs=pl.BlockSpec((tm, tn), lambda i, j: (i, j)),
        out_shape=jax.ShapeDtypeStruct((t, d), F32),
        compiler_params=pltpu.CompilerParams(
            dimension_semantics=("parallel", "arbitrary"), vmem_limit_bytes=VMEM_LIMIT),
        name="out_proj",
    )(x, y_lru, y_rwkv, y_ssd, wa, wb, wc)


def _mlp_kernel(x_ref, g_ref, w1_ref, w2_ref, o_ref, u_ref):
    @pl.when(pl.program_id(1) == 0)
    def _():
        x = x_ref[...]
        u_ref[...] = _bf(_rms(x, g_ref[...]))
        o_ref[...] = x

    h = jnp.dot(u_ref[...], w1_ref[...], preferred_element_type=F32)
    h = jnp.square(jnp.maximum(h, 0.0))
    o_ref[...] += jnp.dot(_bf(h), w2_ref[...], preferred_element_type=F32)


def _mlp(x, g, w1, w2):
    t, d = x.shape
    ff = w1.shape[1]
    tm = min(MM_TM, t)
    tf = 512
    return pl.pallas_call(
        _mlp_kernel,
        grid=(t // tm, ff // tf),
        in_specs=[pl.BlockSpec((tm, d), lambda i, f: (i, 0)),
                  pl.BlockSpec((1, d), lambda i, f: (0, 0)),
                  pl.BlockSpec((d, tf), lambda i, f: (0, f)),
                  pl.BlockSpec((tf, d), lambda i, f: (f, 0))],
        out_specs=pl.BlockSpec((tm, d), lambda i, f: (i, 0)),
        out_shape=jax.ShapeDtypeStruct((t, d), F32),
        scratch_shapes=[pltpu.VMEM((tm, d), BF16)],
        compiler_params=pltpu.CompilerParams(
            dimension_semantics=("parallel", "arbitrary"), vmem_limit_bytes=VMEM_LIMIT),
        name="mlp",
    )(x, g.reshape(1, d), w1, w2)


def _final_norm_kernel(na, x_ref, g_ref, oa_ref, ob_ref):
    y = _rms(x_ref[...], g_ref[...])
    i = pl.program_id(0)

    @pl.when(i < na)
    def _():
        oa_ref[...] = y

    @pl.when(i >= na)
    def _():
        ob_ref[...] = y


def _final_norm(x, g, ta):
    t, d = x.shape
    tm = min(FINAL_TM, ta, t - ta)
    na = ta // tm
    return pl.pallas_call(
        functools.partial(_final_norm_kernel, na),
        grid=(t // tm,),
        in_specs=[pl.BlockSpec((tm, d), lambda i: (i, 0)),
                  pl.BlockSpec((1, d), lambda i: (0, 0))],
        out_specs=[pl.BlockSpec((tm, d), lambda i: (jnp.minimum(i, na - 1), 0)),
                   pl.BlockSpec((tm, d), lambda i: (jnp.maximum(i - na, 0), 0))],
        out_shape=[jax.ShapeDtypeStruct((ta, d), F32), jax.ShapeDtypeStruct((t - ta, d), F32)],
        compiler_params=pltpu.CompilerParams(dimension_semantics=("arbitrary",)),
        name="final_norm",
    )(x, g.reshape(1, d))


def _scan_rows(a, b, reverse):
    n = a.shape[0]
    row = lax.broadcasted_iota(jnp.int32, (n, 1), 0)
    s = 1
    while s < n:
        shift = n - s if reverse else s
        valid = (row < n - s) if reverse else (row >= s)
        a_sh = pltpu.roll(a, shift, 0)
        b_sh = pltpu.roll(b, shift, 0)
        b = jnp.where(valid, a * b_sh, 0.0) + b
        a = jnp.where(valid, a * a_sh, a)
        s *= 2
    return a, b


def _lru_fwd_kernel(segs, tb, p_ref, pprev_ref, pnext_ref, cw_ref, cb_ref, wbd_ref, gb_ref, lam_ref,
                    hf_ref, a1_ref, bx1_ref, carry_ref):
    i = pl.program_id(0)
    pos, slen = _block_pos(i * tb, segs)
    is_start = pos == 0
    is_end = pos + tb == slen
    xb = p_ref[:, D_LRU:]
    prev_row = jnp.where(is_start, 0.0, pprev_ref[SUBLANE - 1:SUBLANE, D_LRU:])
    next0 = jnp.where(is_end, 0.0, pnext_ref[0:1, D_LRU:])
    next1 = jnp.where(is_end, 0.0, pnext_ref[1:2, D_LRU:])
    xc = _conv4(xb, prev_row, next0, next1, cw_ref[...], cb_ref[...])
    gates = _sigmoid(_mm(xc, wbd_ref[...]) + gb_ref[...])
    sp = _softplus(-lam_ref[...])

    def direction(d):
        r = gates[:, (2 * d) * D_LRU:(2 * d + 1) * D_LRU]
        inp = gates[:, (2 * d + 1) * D_LRU:(2 * d + 2) * D_LRU]
        log_a = -LRU_C * r * sp[d:d + 1]
        th = jnp.tanh(log_a)
        one_minus_a2 = -2.0 * th / (1.0 - th)
        return jnp.exp(log_a), jnp.sqrt(one_minus_a2) * (inp * xc)

    a0, bx0 = direction(0)
    a1, bx1 = direction(1)
    a1_ref[...] = a1
    bx1_ref[...] = bx1

    @pl.when(is_start)
    def _():
        carry_ref[...] = jnp.zeros_like(carry_ref)

    a_cum, h_loc = _scan_rows(a0, bx0, False)
    h = h_loc + a_cum * carry_ref[...]
    hf_ref[...] = h
    carry_ref[...] = h[tb - 1:tb]


def _lru_bwd_kernel(segs, tb, nb, p_ref, a1_ref, bx1_ref, hf_ref, o_ref, carry_ref):
    blk = nb - 1 - pl.program_id(0)
    pos, slen = _block_pos(blk * tb, segs)

    @pl.when(pos + tb == slen)
    def _():
        carry_ref[...] = jnp.zeros_like(carry_ref)

    a_cum, h_loc = _scan_rows(a1_ref[...], bx1_ref[...], True)
    h = h_loc + a_cum * carry_ref[...]
    carry_ref[...] = h[0:1]
    o_ref[...] = (hf_ref[...] + h) * _gelu_tanh(p_ref[:, :D_LRU])


def _lru_mixer(p, segs, conv_w, conv_b, w_bd, gate_b, lam):
    t = p.shape[0]
    tb = min(LRU_BLOCK, min(length for _, length in segs))
    nb = t // tb
    prev_spec, next_spec = _halo_specs(tb, LRU_COLS, t // SUBLANE)
    const = lambda shape: pl.BlockSpec(shape, lambda i: (0,) * len(shape))
    row_spec = lambda w: pl.BlockSpec((tb, w), lambda i: (i, 0))
    hf, a1, bx1 = pl.pallas_call(
        functools.partial(_lru_fwd_kernel, segs, tb),
        grid=(nb,),
        in_specs=[row_spec(LRU_COLS), prev_spec, next_spec, const((4, D_LRU)), const((1, D_LRU)),
                  const((D_LRU, 4 * D_LRU)), const((1, 4 * D_LRU)), const((2, D_LRU))],
        out_specs=[row_spec(D_LRU)] * 3,
        out_shape=[jax.ShapeDtypeStruct((t, D_LRU), F32)] * 3,
        scratch_shapes=[pltpu.VMEM((1, D_LRU), F32)],
        compiler_params=pltpu.CompilerParams(
            dimension_semantics=("arbitrary",), vmem_limit_bytes=VMEM_LIMIT),
        name="lru_fwd",
    )(p, p, p, conv_w, conv_b.reshape(1, D_LRU), w_bd, gate_b.reshape(1, 4 * D_LRU), lam)
    rev_spec = lambda w: pl.BlockSpec((tb, w), lambda i: (nb - 1 - i, 0))
    return pl.pallas_call(
        functools.partial(_lru_bwd_kernel, segs, tb, nb),
        grid=(nb,),
        in_specs=[rev_spec(LRU_COLS), rev_spec(D_LRU), rev_spec(D_LRU), rev_spec(D_LRU)],
        out_specs=rev_spec(D_LRU),
        out_shape=jax.ShapeDtypeStruct((t, D_LRU), F32),
        scratch_shapes=[pltpu.VMEM((1, D_LRU), F32)],
        compiler_params=pltpu.CompilerParams(
            dimension_semantics=("arbitrary",), vmem_limit_bytes=VMEM_LIMIT),
        name="lru_bwd",
    )(p, a1, bx1, hf)


def _ssd_core(reverse, xs, bm, cm, dt, da, ht_ref):
    q = xs.shape[0]
    off = SSD_HEADS if reverse else 0
    row = lax.broadcasted_iota(jnp.int32, (q, q), 0)
    col = lax.broadcasted_iota(jnp.int32, (q, q), 1)
    lower = _bf((col <= row).astype(F32))
    cum = _mm_sel_lhs(lower, da)
    total = cum[q - 1:q]
    if reverse:
        u = cum - da
        mask = col >= row
        out_fac = jnp.exp(total - u)
        in_fac = jnp.exp(u)
    else:
        u = cum
        mask = col <= row
        out_fac = jnp.exp(u)
        in_fac = jnp.exp(total - u)
    chunk_decay = jnp.exp(total)
    u_t = u.T
    lane = lax.broadcasted_iota(jnp.int32, (1, LANE), 1)
    first = lane < HEAD_DIM
    cb = []
    for g in range(SSD_GROUPS):
        sl = slice(g * SSD_STATE, (g + 1) * SSD_STATE)
        cb.append(_mm_nt(cm[:, sl], bm[:, sl]))
    ys = []
    for pr in range(SSD_HEADS // 2):
        x_pair = xs[:, pr * LANE:(pr + 1) * LANE]
        ht = ht_ref[pr]
        y_e = []
        h_e = []
        for e in range(2):
            h = 2 * pr + e
            g = h // (SSD_HEADS // SSD_GROUPS)
            c = off + h
            sl = slice(g * SSD_STATE, (g + 1) * SSD_STATE)
            u_col = u[:, c:c + 1]
            u_row = u_t[c:c + 1, :]
            seg = (u_row - u_col) if reverse else (u_col - u_row)
            decay = jnp.exp(jnp.where(mask, seg, -1e30))
            xd = x_pair * dt[:, c:c + 1]
            y = _mm(cb[g] * decay, xd) + _mm(cm[:, sl], ht) * out_fac[:, c:c + 1]
            y_e.append(y)
            h_e.append(chunk_decay[:, c:c + 1] * ht + _mm_tn(bm[:, sl], xd * in_fac[:, c:c + 1]))
        ys.append(jnp.where(first, y_e[0], y_e[1]))
        ht_ref[pr] = jnp.where(first, h_e[0], h_e[1])
    return ys


def _ssd_dt(p_ref, dtb_ref, alog_ref):
    dt = _softplus(p_ref[:, D_SSD + SSD_CONV_DIM:] + dtb_ref[...])
    lane = lax.broadcasted_iota(jnp.int32, (1, LANE), 1)
    a_head = jnp.where(lane < 2 * SSD_HEADS, -jnp.exp(alog_ref[...]), 0.0)
    return dt, dt * a_head


def _ssd_fwd_kernel(segs, q, p_ref, pprev_ref, pnext_ref, cw_ref, cb_ref, dtb_ref, alog_ref, dskip_ref,
                    xconv_ref, yacc_ref, ht_ref):
    i = pl.program_id(0)
    pos, slen = _block_pos(i * q, segs)
    is_start = pos == 0
    is_end = pos + q == slen
    lo, hi = D_SSD, D_SSD + SSD_CONV_DIM
    prev_row = jnp.where(is_start, 0.0, pprev_ref[SUBLANE - 1:SUBLANE, lo:hi])
    next0 = jnp.where(is_end, 0.0, pnext_ref[0:1, lo:hi])
    next1 = jnp.where(is_end, 0.0, pnext_ref[1:2, lo:hi])
    xbc = _silu(_conv4(p_ref[:, lo:hi], prev_row, next0, next1, cw_ref[...], cb_ref[...]))
    xconv_ref[...] = xbc
    xs = xbc[:, :D_SSD]
    dt, da = _ssd_dt(p_ref, dtb_ref, alog_ref)

    @pl.when(is_start)
    def _():
        ht_ref[...] = jnp.zeros_like(ht_ref)

    ys = _ssd_core(False, xs, xbc[:, D_SSD:D_SSD + SSD_BC], xbc[:, D_SSD + SSD_BC:], dt, da, ht_ref)
    for pr, y in enumerate(ys):
        sl = slice(pr * LANE, (pr + 1) * LANE)
        yacc_ref[:, sl] = y + dskip_ref[:, sl] * xs[:, sl]


def _ssd_bwd_kernel(segs, q, nb, p_ref, xconv_ref, yacc_ref, dtb_ref, alog_ref, ng_ref, o_ref, ht_ref):
    blk = nb - 1 - pl.program_id(0)
    pos, slen = _block_pos(blk * q, segs)

    @pl.when(pos + q == slen)
    def _():
        ht_ref[...] = jnp.zeros_like(ht_ref)

    xbc = xconv_ref[...]
    dt, da = _ssd_dt(p_ref, dtb_ref, alog_ref)
    ys = _ssd_core(True, xbc[:, :D_SSD], xbc[:, D_SSD:D_SSD + SSD_BC], xbc[:, D_SSD + SSD_BC:], dt, da, ht_ref)
    y = jnp.concatenate(ys, axis=1) + yacc_ref[...]
    o_ref[...] = _rms(y * _silu(p_ref[:, :D_SSD]), ng_ref[...])


def _ssd_mixer(p, segs, conv_w, conv_b, dt_bias, a_log, d_skip, norm_g):
    t = p.shape[0]
    q = SSD_CHUNK
    nb = t // q
    prev_spec, next_spec = _halo_specs(q, SSD_COLS_PAD, t // SUBLANE)
    const = lambda shape: pl.BlockSpec(shape, lambda i: (0,) * len(shape))
    row_spec = lambda w: pl.BlockSpec((q, w), lambda i: (i, 0))
    state = pltpu.VMEM((SSD_HEADS // 2, SSD_STATE, LANE), F32)
    xconv, yacc = pl.pallas_call(
        functools.partial(_ssd_fwd_kernel, segs, q),
        grid=(nb,),
        in_specs=[row_spec(SSD_COLS_PAD), prev_spec, next_spec, const((4, SSD_CONV_DIM)),
                  const((1, SSD_CONV_DIM)), const((1, LANE)), const((1, LANE)), const((1, D_SSD))],
        out_specs=[row_spec(SSD_CONV_DIM), row_spec(D_SSD)],
        out_shape=[jax.ShapeDtypeStruct((t, SSD_CONV_DIM), F32), jax.ShapeDtypeStruct((t, D_SSD), F32)],
        scratch_shapes=[state],
        compiler_params=pltpu.CompilerParams(
            dimension_semantics=("arbitrary",), vmem_limit_bytes=VMEM_LIMIT),
        name="ssd_fwd",
    )(p, p, p, conv_w, conv_b.reshape(1, SSD_CONV_DIM), dt_bias, a_log, d_skip)
    rev_spec = lambda w: pl.BlockSpec((q, w), lambda i: (nb - 1 - i, 0))
    return pl.pallas_call(
        functools.partial(_ssd_bwd_kernel, segs, q, nb),
        grid=(nb,),
        in_specs=[rev_spec(SSD_COLS_PAD), rev_spec(SSD_CONV_DIM), rev_spec(D_SSD),
                  const((1, LANE)), const((1, LANE)), const((1, D_SSD))],
        out_specs=rev_spec(D_SSD),
        out_shape=jax.ShapeDtypeStruct((t, D_SSD), F32),
        scratch_shapes=[state],
        compiler_params=pltpu.CompilerParams(
            dimension_semantics=("arbitrary",), vmem_limit_bytes=VMEM_LIMIT),
        name="ssd_bwd",
    )(p, xconv, yacc, dt_bias, a_log, norm_g.reshape(1, D_SSD))


def _rwkv_pre_kernel(segs, tb, p_ref, pprev_ref, pnext_ref, mu_ref, ww_ref, wa_ref, w0a0_ref, gup_ref, kk_ref_w,
                     ka_ref, rk_ref, hsel_ref, hselt_ref,
                     r_out, v_out, kk_out, lw0_out, lw1_out, kd0_out, kd1_out, b0_out, b1_out, g_out, bon_out):
    i = pl.program_id(0)
    pos, slen = _block_pos(i * tb, segs)
    p = p_ref[...]
    prev_row = jnp.where(pos == 0, 0.0, pprev_ref[SUBLANE - 1:SUBLANE, :])
    next_row = jnp.where(pos + tb == slen, 0.0, pnext_ref[0:1, :])
    pm1, pp1, _ = _shifted(p, prev_row, next_row, None)
    mu = mu_ref[...]
    ps = p + mu[0:1] * (pm1 - p) + mu[1:2] * (pp1 - p)
    c = D_RWKV
    r = ps[:, 0:c]
    k = ps[:, c:2 * c]
    v = ps[:, 2 * c:3 * c]
    lo0 = 3 * c
    w_win = ps[:, lo0:lo0 + RWKV_LO_WIN]
    a_win = ps[:, lo0 + 4 * RWKV_RANK - RWKV_LO_WIN:lo0 + 4 * RWKV_RANK]
    g_lo = ps[:, lo0 + 4 * RWKV_RANK:lo0 + 4 * RWKV_RANK + RWKV_GATE_RANK]
    pre_w = _mm(jnp.tanh(w_win), ww_ref[...]) + w0a0_ref[:, :2 * c]
    pre_a = _mm(a_win, wa_ref[...]) + w0a0_ref[:, 2 * c:]
    g = _mm(_sigmoid(g_lo), gup_ref[...])
    hsel = hsel_ref[...]
    hselt = hselt_ref[...]
    kk = k * kk_ref_w[...]
    ss = _head_sum(kk * kk, hsel, hselt)
    kk = kk / jnp.maximum(jnp.sqrt(ss), 1e-12)
    kd_sum = None
    outs = ((lw0_out, kd0_out, b0_out), (lw1_out, kd1_out, b1_out))
    for d in range(2):
        w_pre = pre_w[:, d * c:(d + 1) * c]
        a = _sigmoid(pre_a[:, d * c:(d + 1) * c])
        outs[d][0][...] = -math.exp(-0.5) * _sigmoid(w_pre)
        kd = k * (1.0 + (a - 1.0) * ka_ref[...])
        outs[d][1][...] = kd
        outs[d][2][...] = kk * a
        kd_sum = kd if kd_sum is None else kd_sum + kd
    r_out[...] = r
    v_out[...] = v
    kk_out[...] = kk
    g_out[...] = g
    bon_out[...] = _head_sum(r * kd_sum * rk_ref[...], hsel, hselt) * v


def _rwkv_chunk_problems(reverse, cl, r_ref, v_ref, kk_ref, lw_ref, kd_ref, b_ref):
    n2 = 2 * cl
    rowc = lax.broadcasted_iota(jnp.int32, (cl, cl), 0)
    colc = lax.broadcasted_iota(jnp.int32, (cl, cl), 1)
    tri = _bf(((colc >= rowc) if reverse else (colc <= rowc)).astype(F32))
    lw = lw_ref[...]
    cum = _mm_sel_lhs(tri, lw)
    total = cum[0:1] if reverse else cum[cl - 1:cl]
    g_in = jnp.exp(cum)
    g_prev = jnp.exp(cum - lw)
    g_inv = jnp.exp(-cum)
    g_end = jnp.exp(total - cum)
    g_tot = jnp.exp(total)
    kk = kk_ref[...]
    zt = -kk * g_prev
    rt = r_ref[...] * g_in
    bh = b_ref[...] * g_inv
    kh = kd_ref[...] * g_inv
    bg = b_ref[...] * g_end
    kg = kd_ref[...] * g_end
    v = v_ref[...]

    row = lax.broadcasted_iota(jnp.int32, (n2, n2), 0)
    col = lax.broadcasted_iota(jnp.int32, (n2, n2), 1)
    same = (row >= cl) == (col >= cl)
    rt_i = jnp.where(row >= cl, row - cl, row)
    ct_i = jnp.where(col >= cl, col - cl, col)
    if reverse:
        m_strict = same & (ct_i > rt_i)
        m_incl = same & (ct_i >= rt_i)
    else:
        m_strict = same & (ct_i < rt_i)
        m_incl = same & (ct_i <= rt_i)
    eye = row == col
    first = lax.broadcasted_iota(jnp.int32, (1, LANE), 1) < HEAD_DIM
    pair_mask = same & ((rt_i >> 1) == (ct_i >> 1))
    join_masks = []
    shift = 1
    while (1 << shift) < cl:
        later, earlier = (ct_i, rt_i) if reverse else (rt_i, ct_i)
        join_masks.append(same & ((rt_i >> (shift + 1)) == (ct_i >> (shift + 1)))
                          & (((later >> shift) & 1) == 1) & (((earlier >> shift) & 1) == 0))
        shift += 1

    def stack(x):
        return jnp.concatenate([jnp.where(first, x, 0.0), jnp.where(first, 0.0, x)], axis=0)

    masks = dict(strict=m_strict, incl=m_incl, eye=eye, pair=pair_mask, joins=join_masks)
    problems = []
    for pr in range(D_RWKV // LANE):
        sl = slice(pr * LANE, (pr + 1) * LANE)
        ztm, rtm, bhm, khm, bgm, kgm, vm = (_bf(stack(x[:, sl])) for x in (zt, rt, bh, kh, bg, kg, v))
        problems.append(dict(masks, ztm=ztm, rtm=rtm, bhm=bhm, khm=khm, bgm=bgm, kgm=kgm, vm=vm,
                             g_tot=g_tot[:, sl]))
    return problems


def _rwkv_solve(problems, sts):
    ps = range(len(problems))
    n2 = problems[0]['ztm'].shape[0]
    q = problems
    zr = [jnp.concatenate([q[i]['ztm'], q[i]['rtm']], axis=0) for i in ps]
    bk = [jnp.concatenate([q[i]['bhm'], q[i]['khm']], axis=0) for i in ps]
    g = [_mm_nt(zr[i], bk[i]) for i in ps]
    a_ab = [jnp.where(q[i]['strict'], g[i][:n2, :n2], 0.0) for i in ps]
    a_ak = [_bf(jnp.where(q[i]['strict'], g[i][:n2, n2:], 0.0)) for i in ps]
    a_rb = [_bf(jnp.where(q[i]['incl'], g[i][n2:, :n2], 0.0)) for i in ps]
    a_rk = [_bf(jnp.where(q[i]['incl'], g[i][n2:, n2:], 0.0)) for i in ps]
    sa_v = [_mm(a_ak[i], q[i]['vm']) for i in ps]
    y_v = [_mm(a_rk[i], q[i]['vm']) for i in ps]
    q_v = [_mm_tn(q[i]['kgm'], q[i]['vm']) for i in ps]
    tinv = [jnp.where(q[i]['eye'], 1.0, jnp.where(q[i]['pair'], a_ab[i], 0.0)) for i in ps]
    for level in range(len(q[0]['joins'])):
        tb16 = [_bf(tinv[i]) for i in ps]
        half = [_mm(tb16[i], jnp.where(q[i]['joins'][level], a_ab[i], 0.0)) for i in ps]
        tinv = [tinv[i] + _mm(half[i], tb16[i]) for i in ps]
    x = [_bf(_mm(tinv[i], jnp.concatenate([q[i]['ztm'], _bf(sa_v[i])], axis=1))) for i in ps]
    w = [_mm(a_rb[i], x[i]) for i in ps]
    bx = [_mm_tn(q[i]['bgm'], x[i]) for i in ps]
    p_mat = [jnp.where(q[i]['eye'], q[i]['g_tot'], 0.0) + bx[i][:, :LANE] for i in ps]
    q_mat = [bx[i][:, LANE:] + q_v[i] for i in ps]
    r_eff = [q[i]['rtm'].astype(F32) + w[i][:, :LANE] for i in ps]
    ym = [w[i][:, LANE:] + y_v[i] + _mm(r_eff[i], sts[i]) for i in ps]
    st_new = [_mm(p_mat[i], sts[i]) + q_mat[i] for i in ps]
    return ym, st_new


def _rwkv_scan_kernel(segs, cl, nb, rf_ref, vf_ref, kkf_ref, lwf_ref, kdf_ref, bf_ref,
                      rb_ref, vb_ref, kkb_ref, lwb_ref, kdb_ref, bb_ref, yf_ref, yb_ref, stf_ref, stb_ref):
    i = pl.program_id(0)
    pos_f, _ = _block_pos(i * cl, segs)
    pos_b, slen_b = _block_pos((nb - 1 - i) * cl, segs)

    @pl.when(pos_f == 0)
    def _():
        stf_ref[...] = jnp.zeros_like(stf_ref)

    @pl.when(pos_b + cl == slen_b)
    def _():
        stb_ref[...] = jnp.zeros_like(stb_ref)

    npair = D_RWKV // LANE
    problems = (_rwkv_chunk_problems(False, cl, rf_ref, vf_ref, kkf_ref, lwf_ref, kdf_ref, bf_ref)
                + _rwkv_chunk_problems(True, cl, rb_ref, vb_ref, kkb_ref, lwb_ref, kdb_ref, bb_ref))
    sts = [stf_ref[pr] for pr in range(npair)] + [stb_ref[pr] for pr in range(npair)]
    ym, st_new = _rwkv_solve(problems, sts)
    for pr in range(npair):
        sl = slice(pr * LANE, (pr + 1) * LANE)
        stf_ref[pr] = st_new[pr]
        stb_ref[pr] = st_new[npair + pr]
        yf_ref[:, sl] = ym[pr][:cl] + ym[pr][cl:]
        yb_ref[:, sl] = ym[npair + pr][:cl] + ym[npair + pr][cl:]


def _rwkv_post_kernel(yf_ref, yb_ref, bon_ref, g_ref, lng_ref, lnb_ref, hsel_ref, hselt_ref, o_ref):
    y = yf_ref[...] + yb_ref[...]
    hsel = hsel_ref[...]
    hselt = hselt_ref[...]
    mean = _head_sum(y, hsel, hselt) * (1.0 / HEAD_DIM)
    d = y - mean
    var = _head_sum(d * d, hsel, hselt) * (1.0 / HEAD_DIM)
    y = d * lax.rsqrt(var + RWKV_GN_EPS) * lng_ref[...] + lnb_ref[...]
    o_ref[...] = (y + bon_ref[...]) * g_ref[...]


def _rwkv_mixer(p, segs, mu, w_w, w_a, w0a0, g_up, k_k, k_a, r_k, ln_g, ln_b, hsel):
    hselt = hsel.T
    t = p.shape[0]
    c = D_RWKV
    tb = min(TOK_BLOCK, min(length for _, length in segs))
    prev_spec, next_spec = _halo_specs(tb, RWKV_COLS_PAD, t // SUBLANE)
    const = lambda shape: pl.BlockSpec(shape, lambda i: (0,) * len(shape))
    row_spec = lambda n, w: pl.BlockSpec((n, w), lambda i: (i, 0))
    vec = lambda a: a.reshape(1, c)
    tok = jax.ShapeDtypeStruct((t, c), F32)
    r, v, kk, lw0, lw1, kd0, kd1, b0, b1, g, bon = pl.pallas_call(
        functools.partial(_rwkv_pre_kernel, segs, tb),
        grid=(t // tb,),
        in_specs=[row_spec(tb, RWKV_COLS_PAD), prev_spec, next_spec, const((2, RWKV_COLS_PAD)),
                  const((RWKV_LO_WIN, 2 * c)), const((RWKV_LO_WIN, 2 * c)), const((1, 4 * c)),
                  const((RWKV_GATE_RANK, c)), const((1, c)), const((1, c)), const((1, c)),
                  const((c, LANE)), const((LANE, c))],
        out_specs=[row_spec(tb, c)] * 11,
        out_shape=[tok] * 11,
        compiler_params=pltpu.CompilerParams(
            dimension_semantics=("parallel",), vmem_limit_bytes=VMEM_LIMIT),
        name="rwkv_pre",
    )(p, p, p, mu, w_w, w_a, w0a0, g_up, vec(k_k), vec(k_a), vec(r_k), hsel, hselt)

    cl = RWKV_CHUNK
    nb = t // cl
    fwd_spec = pl.BlockSpec((cl, c), lambda i: (i, 0))
    bwd_spec = pl.BlockSpec((cl, c), lambda i: (nb - 1 - i, 0))
    state = pltpu.VMEM((c // LANE, LANE, LANE), F32)
    ys = pl.pallas_call(
        functools.partial(_rwkv_scan_kernel, segs, cl, nb),
        grid=(nb,),
        in_specs=[fwd_spec] * 6 + [bwd_spec] * 6,
        out_specs=[fwd_spec, bwd_spec],
        out_shape=[tok, tok],
        scratch_shapes=[state, state],
        compiler_params=pltpu.CompilerParams(
            dimension_semantics=("arbitrary",), vmem_limit_bytes=VMEM_LIMIT),
        name="rwkv_scan",
    )(r, v, kk, lw0, kd0, b0, r, v, kk, lw1, kd1, b1)

    return pl.pallas_call(
        _rwkv_post_kernel,
        grid=(t // tb,),
        in_specs=[row_spec(tb, c)] * 4 + [const((1, c)), const((1, c)), const((c, LANE)), const((LANE, c))],
        out_specs=row_spec(tb, c),
        out_shape=tok,
        compiler_params=pltpu.CompilerParams(
            dimension_semantics=("parallel",), vmem_limit_bytes=VMEM_LIMIT),
        name="rwkv_post",
    )(ys[0], ys[1], bon, g, vec(ln_g), vec(ln_b), hsel, hselt)


def _pad_cols(w, n):
    return jnp.pad(w, ((0, 0), (0, n - w.shape[1])))


def _lru_gate_blockdiag(gate_w):
    nh = gate_w.shape[2]
    eye = jnp.eye(nh, dtype=gate_w.dtype)
    w = jnp.einsum('dghij,hk->hidgkj', gate_w, eye)
    return w.reshape(D_LRU, 4 * D_LRU)


def _rwkv_lowrank_blocks(up, row0):
    out = jnp.zeros((RWKV_LO_WIN, 2 * D_RWKV), up.dtype)
    for d in range(2):
        out = lax.dynamic_update_slice(out, up[d], (row0 + d * RWKV_RANK, d * D_RWKV))
    return out


def _encoder(x, segs, w):
    (norm1_g, w_in, lru_conv_w, lru_conv_b, lru_gate_w, lru_gate_b, lru_lambda,
     rwkv_mu, rwkv_w0, rwkv_w_up, rwkv_a0, rwkv_a_up, rwkv_g_up, rwkv_k_k, rwkv_k_a,
     rwkv_r_k, rwkv_ln_g, rwkv_ln_b, ssd_conv_w, ssd_conv_b, ssd_dt_bias, ssd_a_log,
     ssd_d, ssd_norm_g, w_out, norm2_g, mlp_w1, mlp_w2, final_norm_g) = w
    depth = w_in.shape[0]
    head_id = jnp.arange(D_RWKV) // HEAD_DIM
    hsel = (head_id[:, None] == jnp.arange(LANE)[None, :]).astype(BF16)
    o1 = LRU_COLS
    o2 = LRU_COLS + RWKV_COLS
    for l in range(depth):
        w_in_l = w_in[l].astype(BF16)
        p_lru = _norm_matmul(x, norm1_g[l], w_in_l[:, :o1], 512)
        p_rwkv = _norm_matmul(x, norm1_g[l], _pad_cols(w_in_l[:, o1:o2], RWKV_COLS_PAD), 512)
        p_ssd = _norm_matmul(x, norm1_g[l], _pad_cols(w_in_l[:, o2:], SSD_COLS_PAD), 384)

        y_lru = _lru_mixer(p_lru, segs, lru_conv_w[l], lru_conv_b[l],
                           _lru_gate_blockdiag(lru_gate_w[l]).astype(BF16), lru_gate_b[l], lru_lambda[l])

        y_rwkv = _rwkv_mixer(
            p_rwkv, segs, _pad_cols(rwkv_mu[l], RWKV_COLS_PAD),
            _rwkv_lowrank_blocks(rwkv_w_up[l], 0).astype(BF16),
            _rwkv_lowrank_blocks(rwkv_a_up[l], RWKV_LO_WIN - 2 * RWKV_RANK).astype(BF16),
            jnp.concatenate([rwkv_w0[l, 0], rwkv_w0[l, 1], rwkv_a0[l, 0], rwkv_a0[l, 1]]).reshape(1, 4 * D_RWKV),
            rwkv_g_up[l].astype(BF16), rwkv_k_k[l], rwkv_k_a[l], rwkv_r_k[l], rwkv_ln_g[l], rwkv_ln_b[l], hsel)

        y_ssd = _ssd_mixer(
            p_ssd, segs, ssd_conv_w[l], ssd_conv_b[l],
            _pad_cols(ssd_dt_bias[l].reshape(1, 2 * SSD_HEADS), LANE),
            _pad_cols(ssd_a_log[l].reshape(1, 2 * SSD_HEADS), LANE),
            jnp.repeat(ssd_d[l], HEAD_DIM).reshape(1, D_SSD), ssd_norm_g[l])

        w_out_l = w_out[l].astype(BF16)
        x = _out_proj(x, y_lru, y_rwkv, y_ssd, w_out_l[:D_LRU], w_out_l[D_LRU:D_LRU + D_RWKV],
                      w_out_l[D_LRU + D_RWKV:])
        x = _mlp(x, norm2_g[l], mlp_w1[l].astype(BF16), mlp_w2[l].astype(BF16))
    return _final_norm(x, final_norm_g, segs[0][0] * segs[0][1])


def kernel(x_prompt, x_sample, norm1_g, w_in, lru_conv_w, lru_conv_b, lru_gate_w, lru_gate_b, lru_lambda, rwkv_mu, rwkv_w0, rwkv_w_up, rwkv_a0, rwkv_a_up, rwkv_g_up, rwkv_k_k, rwkv_k_a, rwkv_r_k, rwkv_ln_g, rwkv_ln_b, ssd_conv_w, ssd_conv_b, ssd_dt_bias, ssd_a_log, ssd_d, ssd_norm_g, w_out, norm2_g, mlp_w1, mlp_w2, final_norm_g):
    weights = (norm1_g, w_in, lru_conv_w, lru_conv_b, lru_gate_w, lru_gate_b, lru_lambda,
               rwkv_mu, rwkv_w0, rwkv_w_up, rwkv_a0, rwkv_a_up, rwkv_g_up, rwkv_k_k, rwkv_k_a,
               rwkv_r_k, rwkv_ln_g, rwkv_ln_b, ssd_conv_w, ssd_conv_b, ssd_dt_bias, ssd_a_log,
               ssd_d, ssd_norm_g, w_out, norm2_g, mlp_w1, mlp_w2, final_norm_g)
    bp, lp, d = x_prompt.shape
    bs, ls, _ = x_sample.shape
    segs = ((bp, lp), (bs, ls))
    x = jnp.concatenate([x_prompt.reshape(bp * lp, d), x_sample.reshape(bs * ls, d)], axis=0)
    y_prompt, y_sample = _encoder(x, segs, weights)
    return y_prompt.reshape(bp, lp, d), y_sample.reshape(bs, ls, d)
```

```python
import functools
import math

import jax
import jax.numpy as jnp
from jax import lax
from jax.experimental import pallas as pl
from jax.experimental.pallas import tpu as pltpu

F32 = jnp.float32
BF16 = jnp.bfloat16
MIX_OUT_DTYPE = BF16

D_MODEL = 2048
D_FF = 4 * D_MODEL
NORM_EPS = 1e-6
HEAD_DIM = 64
D_LRU = 512
LRU_C = 8.0
D_RWKV = 768
RWKV_RANK = 96
RWKV_GATE_RANK = 256
RWKV_LO_WIN = 256
RWKV_GN_EPS = 64e-5
RWKV_COLS = 3 * D_RWKV + 4 * RWKV_RANK + RWKV_GATE_RANK
RWKV_COLS_PAD = 3072
D_SSD = 768
SSD_HEADS = 12
SSD_GROUPS = 4
SSD_STATE = 128
SSD_BC = SSD_GROUPS * SSD_STATE
SSD_CONV_DIM = D_SSD + 2 * SSD_BC
SSD_COLS_PAD = 3072
SSD_DT_OFF = SSD_CONV_DIM
SSD_Z_OFF = SSD_COLS_PAD - D_SSD
LRU_COLS = 2 * D_LRU

LANE = 128
SUBLANE = 8
VMEM_LIMIT = 56 * 1024 * 1024

TOK_BLOCK = 256
LRU_BLOCK = 512
SSD_CHUNK = 128
SSD_DECAY_ROWS = 1024
SSD_FAC_COLS = 5 * LANE
RWKV_CHUNK = 64
MM_TM = 1024
IN_PROJ_TN = 512
MLP_TF = 512
FINAL_TM = 512


def _bf(x):
    return x.astype(BF16)


def _mm(a, b):
    return jnp.dot(_bf(a), _bf(b), preferred_element_type=F32)


def _mm_nt(a, b):
    return lax.dot_general(_bf(a), _bf(b), (((1,), (1,)), ((), ())), preferred_element_type=F32)


def _mm_tn(a, b):
    return lax.dot_general(_bf(a), _bf(b), (((0,), (0,)), ((), ())), preferred_element_type=F32)


def _split3(x):
    x1 = _bf(x)
    r1 = x - x1.astype(F32)
    x2 = _bf(r1)
    x3 = _bf(r1 - x2.astype(F32))
    return x1, x2, x3


def _mm_sel_lhs(sel, x):
    x1, x2, x3 = _split3(x)
    d = lambda v: jnp.dot(sel, v, preferred_element_type=F32)
    return d(x1) + d(x2) + d(x3)


def _mm_sel_rhs2(x, sel):
    x1 = _bf(x)
    x2 = _bf(x - x1.astype(F32))
    return jnp.dot(x1, sel, preferred_element_type=F32) + jnp.dot(x2, sel, preferred_element_type=F32)


def _head_sum(x, sel, sel_t):
    return _mm_sel_rhs2(_mm_sel_rhs2(x, sel), sel_t)


def _sigmoid(x):
    return 0.5 * jnp.tanh(0.5 * x) + 0.5


def _softplus(x):
    return jnp.maximum(x, 0.0) + jnp.log1p(jnp.exp(-jnp.abs(x)))


def _silu(x):
    return x * _sigmoid(x)


def _gelu_tanh(x):
    return 0.5 * x * (1.0 + jnp.tanh(math.sqrt(2.0 / math.pi) * (x + 0.044715 * (x * x * x))))


def _rms(x, g):
    return x * lax.rsqrt(jnp.mean(x * x, axis=-1, keepdims=True) + NORM_EPS) * g


def _block_pos(t0, segs):
    pos = None
    slen = None
    base = 0
    for n, length in segs:
        p = lax.rem(t0 - base, length)
        if pos is None:
            pos, slen = p, jnp.int32(length)
        else:
            inside = t0 >= base
            pos = jnp.where(inside, p, pos)
            slen = jnp.where(inside, jnp.int32(length), slen)
        base += n * length
    return pos, slen


def _total_tokens(segs):
    return sum(n * length for n, length in segs)


def _halo_specs(tb, width, nblocks_8, rev_nb=None):
    r = tb // SUBLANE
    if rev_nb is None:
        blk = lambda i: i
    else:
        blk = lambda i: rev_nb - 1 - i
    prev = pl.BlockSpec((SUBLANE, width), lambda i: (jnp.maximum(blk(i) * r - 1, 0), 0))
    nxt = pl.BlockSpec((SUBLANE, width), lambda i: (jnp.minimum((blk(i) + 1) * r, nblocks_8 - 1), 0))
    return prev, nxt


def _shifted(x, prev_row, next0, next1):
    n = x.shape[0]
    row = lax.broadcasted_iota(jnp.int32, (n, 1), 0)
    xm1 = jnp.where(row == 0, prev_row, pltpu.roll(x, 1, 0))
    xp1 = jnp.where(row == n - 1, next0, pltpu.roll(x, n - 1, 0))
    if next1 is None:
        return xm1, xp1, None
    xp2 = jnp.where(row == n - 2, next0, jnp.where(row == n - 1, next1, pltpu.roll(x, n - 2, 0)))
    return xm1, xp1, xp2


def _conv4(x, prev_row, next0, next1, w, b):
    xm1, xp1, xp2 = _shifted(x, prev_row, next0, next1)
    return w[0:1] * xm1 + w[1:2] * x + w[2:3] * xp1 + w[3:4] * xp2 + b


def _in_proj_kernel(n_lru, n_rwkv, x_ref, g_ref, w_ref, lru_ref, rwkv_ref, ssd_ref, u_ref):
    j = pl.program_id(1)

    @pl.when(j == 0)
    def _():
        u_ref[...] = _bf(_rms(x_ref[...], g_ref[...]))

    def project(o_ref):
        o_ref[...] = jnp.dot(u_ref[...], w_ref[...], preferred_element_type=F32)

    pl.when(j < n_lru)(functools.partial(project, lru_ref))
    pl.when((j >= n_lru) & (j < n_lru + n_rwkv))(functools.partial(project, rwkv_ref))
    pl.when(j >= n_lru + n_rwkv)(functools.partial(project, ssd_ref))


def _in_proj(x, g, w):
    t, d = x.shape
    tm = min(MM_TM, t)
    tn = IN_PROJ_TN
    n_lru, n_rwkv, n_ssd = LRU_COLS // tn, RWKV_COLS_PAD // tn, SSD_COLS_PAD // tn
    return pl.pallas_call(
        functools.partial(_in_proj_kernel, n_lru, n_rwkv),
        grid=(t // tm, n_lru + n_rwkv + n_ssd),
        in_specs=[pl.BlockSpec((tm, d), lambda i, j: (i, 0)),
                  pl.BlockSpec((1, d), lambda i, j: (0, 0)),
                  pl.BlockSpec((d, tn), lambda i, j: (0, j))],
        out_specs=[pl.BlockSpec((tm, tn), lambda i, j: (i, jnp.minimum(j, n_lru - 1))),
                   pl.BlockSpec((tm, tn), lambda i, j: (i, jnp.clip(j - n_lru, 0, n_rwkv - 1))),
                   pl.BlockSpec((tm, tn), lambda i, j: (i, jnp.maximum(j - n_lru - n_rwkv, 0)))],
        out_shape=[jax.ShapeDtypeStruct((t, LRU_COLS), F32),
                   jax.ShapeDtypeStruct((t, RWKV_COLS_PAD), F32),
                   jax.ShapeDtypeStruct((t, SSD_COLS_PAD), F32)],
        scratch_shapes=[pltpu.VMEM((tm, d), BF16)],
        compiler_params=pltpu.CompilerParams(
            dimension_semantics=("arbitrary", "arbitrary"), vmem_limit_bytes=VMEM_LIMIT),
        name="in_proj",
    )(x, g.reshape(1, d), w)


def _out_proj_kernel(x_ref, ya_ref, yb_ref, yc_ref, wa_ref, wb_ref, wc_ref, o_ref):
    acc = jnp.dot(_bf(ya_ref[...]), wa_ref[...], preferred_element_type=F32)
    acc += jnp.dot(_bf(yb_ref[...]), wb_ref[...], preferred_element_type=F32)
    acc += jnp.dot(_bf(yc_ref[...]), wc_ref[...], preferred_element_type=F32)
    o_ref[...] = x_ref[...] + acc


def _out_proj(x, y_lru, y_rwkv, y_ssd, wa, wb, wc):
    t, d = x.shape
    tm = min(MM_TM, t)
    tn = 1024
    return pl.pallas_call(
        _out_proj_kernel,
        grid=(t // tm, d // tn),
        in_specs=[pl.BlockSpec((tm, tn), lambda i, j: (i, j)),
                  pl.BlockSpec((tm, D_LRU), lambda i, j: (i, 0)),
                  pl.BlockSpec((tm, D_RWKV), lambda i, j: (i, 0)),
                  pl.BlockSpec((tm, D_SSD), lambda i, j: (i, 0)),
                  pl.BlockSpec((D_LRU, tn), lambda i, j: (0, j)),
                  pl.BlockSpec((D_RWKV, tn), lambda i, j: (0, j)),
                  pl.BlockSpec((D_SSD, tn), lambda i, j: (0, j))],
        out_specs=pl.BlockSpec((tm, tn), lambda i, j: (i, j)),
        out_shape=jax.ShapeDtypeStruct((t, d), F32),
        compiler_params=pltpu.CompilerParams(
            dimension_semantics=("parallel", "arbitrary"), vmem_limit_bytes=VMEM_LIMIT),
        name="out_proj",
    )(x, y_lru, y_rwkv, y_ssd, wa, wb, wc)


def _mlp_kernel(x_ref, g_ref, w1_ref, w2_ref, o_ref, u_ref):
    @pl.when(pl.program_id(1) == 0)
    def _():
        x = x_ref[...]
        u_ref[...] = _bf(_rms(x, g_ref[...]))
        o_ref[...] = x

    h = jnp.dot(u_ref[...], w1_ref[...], preferred_element_type=F32)
    h = jnp.square(jnp.maximum(h, 0.0))
    o_ref[...] += jnp.dot(_bf(h), w2_ref[...], preferred_element_type=F32)


def _mlp(x, g, w1, w2):
    t, d = x.shape
    ff = w1.shape[1]
    tm = min(MM_TM, t)
    tf = MLP_TF
    return pl.pallas_call(
        _mlp_kernel,
        grid=(t // tm, ff // tf),
        in_specs=[pl.BlockSpec((tm, d), lambda i, f: (i, 0)),
                  pl.BlockSpec((1, d), lambda i, f: (0, 0)),
                  pl.BlockSpec((d, tf), lambda i, f: (0, f)),
                  pl.BlockSpec((tf, d), lambda i, f: (f, 0))],
        out_specs=pl.BlockSpec((tm, d), lambda i, f: (i, 0)),
        out_shape=jax.ShapeDtypeStruct((t, d), F32),
        scratch_shapes=[pltpu.VMEM((tm, d), BF16)],
        compiler_params=pltpu.CompilerParams(
            dimension_semantics=("parallel", "arbitrary"), vmem_limit_bytes=VMEM_LIMIT),
        name="mlp",
    )(x, g.reshape(1, d), w1, w2)


def _final_norm_kernel(na, x_ref, g_ref, oa_ref, ob_ref):
    y = _rms(x_ref[...], g_ref[...])
    i = pl.program_id(0)

    @pl.when(i < na)
    def _():
        oa_ref[...] = y

    @pl.when(i >= na)
    def _():
        ob_ref[...] = y


def _final_norm(x, g, ta):
    t, d = x.shape
    tm = min(FINAL_TM, ta, t - ta)
    na = ta // tm
    return pl.pallas_call(
        functools.partial(_final_norm_kernel, na),
        grid=(t // tm,),
        in_specs=[pl.BlockSpec((tm, d), lambda i: (i, 0)),
                  pl.BlockSpec((1, d), lambda i: (0, 0))],
        out_specs=[pl.BlockSpec((tm, d), lambda i: (jnp.minimum(i, na - 1), 0)),
                   pl.BlockSpec((tm, d), lambda i: (jnp.maximum(i - na, 0), 0))],
        out_shape=[jax.ShapeDtypeStruct((ta, d), F32), jax.ShapeDtypeStruct((t - ta, d), F32)],
        compiler_params=pltpu.CompilerParams(dimension_semantics=("arbitrary",)),
        name="final_norm",
    )(x, g.reshape(1, d))


def _scan_rows(a, b, reverse):
    n = a.shape[0]
    row = lax.broadcasted_iota(jnp.int32, (n, 1), 0)
    s = 1
    while s < n:
        shift = n - s if reverse else s
        valid = (row < n - s) if reverse else (row >= s)
        a_sh = pltpu.roll(a, shift, 0)
        b_sh = pltpu.roll(b, shift, 0)
        b = jnp.where(valid, a * b_sh, 0.0) + b
        a = jnp.where(valid, a * a_sh, a)
        s *= 2
    return a, b


def _scan_block(a, b, carry, reverse):
    tiles = range(a.shape[0] // SUBLANE)
    loc = [_scan_rows(a[k * SUBLANE:(k + 1) * SUBLANE], b[k * SUBLANE:(k + 1) * SUBLANE], reverse) for k in tiles]
    hs = [None] * len(tiles)
    for k in (reversed(tiles) if reverse else tiles):
        a_cum, h_loc = loc[k]
        h = h_loc + a_cum * carry
        carry = h[0:1] if reverse else h[SUBLANE - 1:SUBLANE]
        hs[k] = h
    return jnp.concatenate(hs, axis=0), carry


def _lru_fwd_kernel(segs, tb, p_ref, pprev_ref, pnext_ref, cw_ref, cb_ref, wbd_ref, gb_ref, lam_ref,
                    hf_ref, a1_ref, bx1_ref, carry_ref):
    i = pl.program_id(0)
    pos, slen = _block_pos(i * tb, segs)
    is_start = pos == 0
    is_end = pos + tb == slen
    xb = p_ref[:, D_LRU:]
    prev_row = jnp.where(is_start, 0.0, pprev_ref[SUBLANE - 1:SUBLANE, D_LRU:])
    next0 = jnp.where(is_end, 0.0, pnext_ref[0:1, D_LRU:])
    next1 = jnp.where(is_end, 0.0, pnext_ref[1:2, D_LRU:])
    xc = _conv4(xb, prev_row, next0, next1, cw_ref[...], cb_ref[...])
    gates = _sigmoid(_mm(xc, wbd_ref[...]) + gb_ref[...])
    sp = _softplus(-lam_ref[...])

    def direction(d):
        r = gates[:, (2 * d) * D_LRU:(2 * d + 1) * D_LRU]
        inp = gates[:, (2 * d + 1) * D_LRU:(2 * d + 2) * D_LRU]
        log_a = -LRU_C * r * sp[d:d + 1]
        th = jnp.tanh(log_a)
        one_minus_a2 = -2.0 * th / (1.0 - th)
        return jnp.exp(log_a), jnp.sqrt(one_minus_a2) * (inp * xc)

    a0, bx0 = direction(0)
    a1, bx1 = direction(1)
    a1_ref[...] = a1
    bx1_ref[...] = bx1

    @pl.when(is_start)
    def _():
        carry_ref[...] = jnp.zeros_like(carry_ref)

    h, carry = _scan_block(a0, bx0, carry_ref[...], False)
    hf_ref[...] = h
    carry_ref[...] = carry


def _lru_bwd_kernel(segs, tb, nb, p_ref, a1_ref, bx1_ref, hf_ref, o_ref, carry_ref):
    blk = nb - 1 - pl.program_id(0)
    pos, slen = _block_pos(blk * tb, segs)

    @pl.when(pos + tb == slen)
    def _():
        carry_ref[...] = jnp.zeros_like(carry_ref)

    h, carry = _scan_block(a1_ref[...], bx1_ref[...], carry_ref[...], True)
    carry_ref[...] = carry
    o_ref[...] = ((hf_ref[...] + h) * _gelu_tanh(p_ref[:, :D_LRU])).astype(o_ref.dtype)


def _lru_mixer(p, segs, conv_w, conv_b, w_bd, gate_b, lam):
    t = p.shape[0]
    tb = min(LRU_BLOCK, min(length for _, length in segs))
    nb = t // tb
    prev_spec, next_spec = _halo_specs(tb, LRU_COLS, t // SUBLANE)
    const = lambda shape: pl.BlockSpec(shape, lambda i: (0,) * len(shape))
    row_spec = lambda w: pl.BlockSpec((tb, w), lambda i: (i, 0))
    hf, a1, bx1 = pl.pallas_call(
        functools.partial(_lru_fwd_kernel, segs, tb),
        grid=(nb,),
        in_specs=[row_spec(LRU_COLS), prev_spec, next_spec, const((4, D_LRU)), const((1, D_LRU)),
                  const((D_LRU, 4 * D_LRU)), const((1, 4 * D_LRU)), const((2, D_LRU))],
        out_specs=[row_spec(D_LRU)] * 3,
        out_shape=[jax.ShapeDtypeStruct((t, D_LRU), F32)] * 3,
        scratch_shapes=[pltpu.VMEM((1, D_LRU), F32)],
        compiler_params=pltpu.CompilerParams(
            dimension_semantics=("arbitrary",), vmem_limit_bytes=VMEM_LIMIT),
        name="lru_fwd",
    )(p, p, p, conv_w, conv_b.reshape(1, D_LRU), w_bd, gate_b.reshape(1, 4 * D_LRU), lam)
    rev_spec = lambda w: pl.BlockSpec((tb, w), lambda i: (nb - 1 - i, 0))
    return pl.pallas_call(
        functools.partial(_lru_bwd_kernel, segs, tb, nb),
        grid=(nb,),
        in_specs=[rev_spec(LRU_COLS), rev_spec(D_LRU), rev_spec(D_LRU), rev_spec(D_LRU)],
        out_specs=rev_spec(D_LRU),
        out_shape=jax.ShapeDtypeStruct((t, D_LRU), MIX_OUT_DTYPE),
        scratch_shapes=[pltpu.VMEM((1, D_LRU), F32)],
        compiler_params=pltpu.CompilerParams(
            dimension_semantics=("arbitrary",), vmem_limit_bytes=VMEM_LIMIT),
        name="lru_bwd",
    )(p, a1, bx1, hf)


def _ssd_core(reverse, xs, bm, cm, fac, u_t, hexp, ht_ref):
    q = xs.shape[0]
    off = SSD_HEADS if reverse else 0
    row = lax.broadcasted_iota(jnp.int32, (q, q), 0)
    col = lax.broadcasted_iota(jnp.int32, (q, q), 1)
    mask = (col >= row) if reverse else (col <= row)
    dt, u, out_fac, in_fac = (fac[:, n * LANE:(n + 1) * LANE] for n in range(4))
    chunk_decay = fac[0:1, 4 * LANE:5 * LANE]
    lane = lax.broadcasted_iota(jnp.int32, (1, LANE), 1)
    first = lane < HEAD_DIM
    heads = range(SSD_HEADS)
    pairs = range(SSD_HEADS // 2)
    grp = [h // (SSD_HEADS // SSD_GROUPS) for h in heads]
    cols = [slice(off + h, off + h + 1) for h in heads]
    gsl = [slice(g * SSD_STATE, (g + 1) * SSD_STATE) for g in range(SSD_GROUPS)]
    cm_g = [_bf(cm[:, sl]) for sl in gsl]
    bm_g = [_bf(bm[:, sl]) for sl in gsl]
    cb = [_mm_nt(cm_g[g], bm_g[g]) for g in range(SSD_GROUPS)]
    hts = [ht_ref[pr] for pr in pairs]
    hts_bf = [_bf(ht) for ht in hts]
    x_pair = [xs[:, pr * LANE:(pr + 1) * LANE] for pr in pairs]
    seg = [(u_t[c, :] - u[:, c]) if reverse else (u[:, c] - u_t[c, :]) for c in cols]
    m = [_bf(cb[grp[h]] * jnp.exp(jnp.where(mask, seg[h], -1e30))) for h in heads]
    dt_x = _mm_sel_rhs2(dt, hexp)
    dt_in_x = _mm_sel_rhs2(dt * in_fac, hexp)
    out_x = _mm_sel_rhs2(out_fac, hexp)
    psl = [slice(pr * LANE, (pr + 1) * LANE) for pr in pairs]
    xd = [_bf(x_pair[pr] * dt_x[:, psl[pr]]) for pr in pairs]
    xd_in = [_bf(x_pair[pr] * dt_in_x[:, psl[pr]]) for pr in pairs]
    y_diag = [_mm(m[h], xd[h // 2]) for h in heads]
    c_ht = {key: _mm(cm_g[key[1]], hts_bf[key[0]]) for key in sorted({(h // 2, grp[h]) for h in heads})}
    h_new = [chunk_decay[:, cols[h]] * hts[h // 2] + _mm_tn(bm_g[grp[h]], xd_in[h // 2]) for h in heads]
    for pr in pairs:
        ht_ref[pr] = jnp.where(first, h_new[2 * pr], h_new[2 * pr + 1])
    pick = lambda pr, vals: jnp.where(first, vals[0], vals[1])
    return [pick(pr, (y_diag[2 * pr], y_diag[2 * pr + 1]))
            + pick(pr, (c_ht[(pr, grp[2 * pr])], c_ht[(pr, grp[2 * pr + 1])])) * out_x[:, psl[pr]] for pr in pairs]


def _ssd_decay_kernel(q, dtraw_ref, dtb_ref, alog_ref, fac_ref, ut_ref):
    ks = range(dtraw_ref.shape[0] // q)
    rows = [slice(k * q, (k + 1) * q) for k in ks]
    lane = lax.broadcasted_iota(jnp.int32, (1, LANE), 1)
    fwd_lane = lane < SSD_HEADS
    a_head = jnp.where(lane < 2 * SSD_HEADS, -jnp.exp(alog_ref[...]), 0.0)
    row = lax.broadcasted_iota(jnp.int32, (q, q), 0)
    col = lax.broadcasted_iota(jnp.int32, (q, q), 1)
    lower = _bf((col <= row).astype(F32))
    dt = [_softplus(dtraw_ref[r, :] + dtb_ref[...]) for r in rows]
    da = [dt[k] * a_head for k in ks]
    cum = [_mm_sel_lhs(lower, da[k]) for k in ks]
    total = [cum[k][q - 1:q] for k in ks]
    u = [jnp.where(fwd_lane, cum[k], cum[k] - da[k]) for k in ks]
    e_u = [jnp.exp(u[k]) for k in ks]
    e_rest = [jnp.exp(total[k] - u[k]) for k in ks]
    for k in ks:
        pieces = (dt[k], u[k], jnp.where(fwd_lane, e_u[k], e_rest[k]), jnp.where(fwd_lane, e_rest[k], e_u[k]),
                  jnp.broadcast_to(jnp.exp(total[k]), (q, LANE)))
        for n, piece in enumerate(pieces):
            fac_ref[rows[k], n * LANE:(n + 1) * LANE] = piece
        ut_ref[:, rows[k]] = u[k].T


def _ssd_fwd_kernel(segs, q, p_ref, pprev_ref, pnext_ref, cw_ref, cb_ref, fac_ref, ut_ref, hexp_ref, dskip_ref,
                    xconv_ref, yacc_ref, ht_ref):
    i = pl.program_id(0)
    pos, slen = _block_pos(i * q, segs)
    is_start = pos == 0
    is_end = pos + q == slen
    prev_row = jnp.where(is_start, 0.0, pprev_ref[SUBLANE - 1:SUBLANE, :])
    next0 = jnp.where(is_end, 0.0, pnext_ref[0:1, :])
    next1 = jnp.where(is_end, 0.0, pnext_ref[1:2, :])
    xbc = _silu(_conv4(p_ref[...], prev_row, next0, next1, cw_ref[...], cb_ref[...]))
    xconv_ref[...] = xbc
    xs = xbc[:, :D_SSD]

    @pl.when(is_start)
    def _():
        ht_ref[...] = jnp.zeros_like(ht_ref)

    ys = _ssd_core(False, xs, xbc[:, D_SSD:D_SSD + SSD_BC], xbc[:, D_SSD + SSD_BC:], fac_ref[...], ut_ref[...],
                   hexp_ref[...], ht_ref)
    for pr, y in enumerate(ys):
        sl = slice(pr * LANE, (pr + 1) * LANE)
        yacc_ref[:, sl] = y + dskip_ref[:, sl] * xs[:, sl]


def _ssd_bwd_kernel(segs, q, nb, z_ref, xconv_ref, yacc_ref, fac_ref, ut_ref, hexp_ref, ng_ref, o_ref, ht_ref):
    blk = nb - 1 - pl.program_id(0)
    pos, slen = _block_pos(blk * q, segs)

    @pl.when(pos + q == slen)
    def _():
        ht_ref[...] = jnp.zeros_like(ht_ref)

    xbc = xconv_ref[...]
    ys = _ssd_core(True, xbc[:, :D_SSD], xbc[:, D_SSD:D_SSD + SSD_BC], xbc[:, D_SSD + SSD_BC:], fac_ref[...],
                   ut_ref[...], hexp_ref[...], ht_ref)
    y = jnp.concatenate(ys, axis=1) + yacc_ref[...]
    o_ref[...] = _rms(y * _silu(z_ref[...]), ng_ref[...]).astype(o_ref.dtype)


def _ssd_mixer(p, segs, conv_w, conv_b, dt_bias, a_log, d_skip, norm_g):
    t = p.shape[0]
    q = SSD_CHUNK
    nb = t // q
    prev_spec, next_spec = _halo_specs(q, SSD_CONV_DIM, t // SUBLANE)
    const = lambda shape: pl.BlockSpec(shape, lambda i: (0,) * len(shape))
    row_spec = lambda w: pl.BlockSpec((q, w), lambda i: (i, 0))
    state = pltpu.VMEM((SSD_HEADS // 2, SSD_STATE, LANE), F32)
    rows = min(SSD_DECAY_ROWS, t)
    dt_col = SSD_DT_OFF // LANE
    head_expand = lambda off: (jnp.arange(LANE)[:, None] == off + jnp.arange(D_SSD)[None, :] // HEAD_DIM).astype(BF16)
    fac, u_t = pl.pallas_call(
        functools.partial(_ssd_decay_kernel, q),
        grid=(t // rows,),
        in_specs=[pl.BlockSpec((rows, LANE), lambda i: (i, dt_col)), const((1, LANE)), const((1, LANE))],
        out_specs=[pl.BlockSpec((rows, SSD_FAC_COLS), lambda i: (i, 0)), pl.BlockSpec((LANE, rows), lambda i: (0, i))],
        out_shape=[jax.ShapeDtypeStruct((t, SSD_FAC_COLS), F32), jax.ShapeDtypeStruct((LANE, t), F32)],
        compiler_params=pltpu.CompilerParams(
            dimension_semantics=("parallel",), vmem_limit_bytes=VMEM_LIMIT),
        name="ssd_decay",
    )(p, dt_bias, a_log)
    xconv, yacc = pl.pallas_call(
        functools.partial(_ssd_fwd_kernel, segs, q),
        grid=(nb,),
        in_specs=[row_spec(SSD_CONV_DIM), prev_spec, next_spec, const((4, SSD_CONV_DIM)),
                  const((1, SSD_CONV_DIM)), row_spec(SSD_FAC_COLS), pl.BlockSpec((LANE, q), lambda i: (0, i)),
                  const((LANE, D_SSD)), const((1, D_SSD))],
        out_specs=[row_spec(SSD_CONV_DIM), row_spec(D_SSD)],
        out_shape=[jax.ShapeDtypeStruct((t, SSD_CONV_DIM), F32), jax.ShapeDtypeStruct((t, D_SSD), F32)],
        scratch_shapes=[state],
        compiler_params=pltpu.CompilerParams(
            dimension_semantics=("arbitrary",), vmem_limit_bytes=VMEM_LIMIT),
        name="ssd_fwd",
    )(p, p, p, conv_w, conv_b.reshape(1, SSD_CONV_DIM), fac, u_t, head_expand(0), d_skip)
    rev_spec = lambda w: pl.BlockSpec((q, w), lambda i: (nb - 1 - i, 0))
    return pl.pallas_call(
        functools.partial(_ssd_bwd_kernel, segs, q, nb),
        grid=(nb,),
        in_specs=[pl.BlockSpec((q, D_SSD), lambda i: (nb - 1 - i, SSD_Z_OFF // D_SSD)),
                  rev_spec(SSD_CONV_DIM), rev_spec(D_SSD), rev_spec(SSD_FAC_COLS),
                  pl.BlockSpec((LANE, q), lambda i: (0, nb - 1 - i)), const((LANE, D_SSD)), const((1, D_SSD))],
        out_specs=rev_spec(D_SSD),
        out_shape=jax.ShapeDtypeStruct((t, D_SSD), MIX_OUT_DTYPE),
        scratch_shapes=[state],
        compiler_params=pltpu.CompilerParams(
            dimension_semantics=("arbitrary",), vmem_limit_bytes=VMEM_LIMIT),
        name="ssd_bwd",
    )(p, xconv, yacc, fac, u_t, head_expand(SSD_HEADS), norm_g.reshape(1, D_SSD))


def _rwkv_pre_kernel(segs, tb, p_ref, pprev_ref, pnext_ref, mu_ref, ww_ref, wa_ref, w0a0_ref, gup_ref, kk_ref_w,
                     ka_ref, rk_ref, hsel_ref, hselt_ref,
                     r_out, v_out, kk_out, lw0_out, lw1_out, kd0_out, kd1_out, b0_out, b1_out, g_out, bon_out):
    i = pl.program_id(0)
    pos, slen = _block_pos(i * tb, segs)
    p = p_ref[...]
    prev_row = jnp.where(pos == 0, 0.0, pprev_ref[SUBLANE - 1:SUBLANE, :])
    next_row = jnp.where(pos + tb == slen, 0.0, pnext_ref[0:1, :])
    pm1, pp1, _ = _shifted(p, prev_row, next_row, None)
    mu = mu_ref[...]
    ps = p + mu[0:1] * (pm1 - p) + mu[1:2] * (pp1 - p)
    c = D_RWKV
    r = ps[:, 0:c]
    k = ps[:, c:2 * c]
    v = ps[:, 2 * c:3 * c]
    lo0 = 3 * c
    w_win = ps[:, lo0:lo0 + RWKV_LO_WIN]
    a_win = ps[:, lo0 + 4 * RWKV_RANK - RWKV_LO_WIN:lo0 + 4 * RWKV_RANK]
    g_lo = ps[:, lo0 + 4 * RWKV_RANK:lo0 + 4 * RWKV_RANK + RWKV_GATE_RANK]
    pre_w = _mm(jnp.tanh(w_win), ww_ref[...]) + w0a0_ref[:, :2 * c]
    pre_a = _mm(a_win, wa_ref[...]) + w0a0_ref[:, 2 * c:]
    g = _mm(_sigmoid(g_lo), gup_ref[...])
    hsel = hsel_ref[...]
    hselt = hselt_ref[...]
    kk = k * kk_ref_w[...]
    ss = _head_sum(kk * kk, hsel, hselt)
    kk = kk / jnp.maximum(jnp.sqrt(ss), 1e-12)
    kd_sum = None
    outs = ((lw0_out, kd0_out, b0_out), (lw1_out, kd1_out, b1_out))
    for d in range(2):
        w_pre = pre_w[:, d * c:(d + 1) * c]
        a = _sigmoid(pre_a[:, d * c:(d + 1) * c])
        outs[d][0][...] = -math.exp(-0.5) * _sigmoid(w_pre)
        kd = k * (1.0 + (a - 1.0) * ka_ref[...])
        outs[d][1][...] = kd.astype(outs[d][1].dtype)
        outs[d][2][...] = (kk * a).astype(outs[d][2].dtype)
        kd_sum = kd if kd_sum is None else kd_sum + kd
    r_out[...] = r.astype(r_out.dtype)
    v_out[...] = v.astype(v_out.dtype)
    kk_out[...] = kk.astype(kk_out.dtype)
    g_out[...] = g
    bon_out[...] = _head_sum(r * kd_sum * rk_ref[...], hsel, hselt) * v


def _rwkv_chunk_problems(reverse, cl, r_ref, v_ref, kk_ref, lw_ref, kd_ref, b_ref):
    n2 = 2 * cl
    rowc = lax.broadcasted_iota(jnp.int32, (cl, cl), 0)
    colc = lax.broadcasted_iota(jnp.int32, (cl, cl), 1)
    tri = _bf(((colc >= rowc) if reverse else (colc <= rowc)).astype(F32))
    lw = lw_ref[...]
    cum = _mm_sel_lhs(tri, lw)
    total = cum[0:1] if reverse else cum[cl - 1:cl]
    g_in = jnp.exp(cum)
    g_prev = jnp.exp(cum - lw)
    g_inv = jnp.exp(-cum)
    g_end = jnp.exp(total - cum)
    g_tot = jnp.exp(total)
    kk = kk_ref[...]
    zt = -kk * g_prev
    rt = r_ref[...] * g_in
    bh = b_ref[...] * g_inv
    kh = kd_ref[...] * g_inv
    bg = b_ref[...] * g_end
    kg = kd_ref[...] * g_end
    v = v_ref[...]

    row = lax.broadcasted_iota(jnp.int32, (n2, n2), 0)
    col = lax.broadcasted_iota(jnp.int32, (n2, n2), 1)
    same = (row >= cl) == (col >= cl)
    rt_i = jnp.where(row >= cl, row - cl, row)
    ct_i = jnp.where(col >= cl, col - cl, col)
    if reverse:
        m_strict = same & (ct_i > rt_i)
        m_incl = same & (ct_i >= rt_i)
    else:
        m_strict = same & (ct_i < rt_i)
        m_incl = same & (ct_i <= rt_i)
    eye = row == col
    first = lax.broadcasted_iota(jnp.int32, (1, LANE), 1) < HEAD_DIM
    pair_mask = same & ((rt_i >> 1) == (ct_i >> 1))
    join_masks = []
    shift = 1
    while (1 << shift) < cl:
        later, earlier = (ct_i, rt_i) if reverse else (rt_i, ct_i)
        join_masks.append(same & ((rt_i >> (shift + 1)) == (ct_i >> (shift + 1)))
                          & (((later >> shift) & 1) == 1) & (((earlier >> shift) & 1) == 0))
        shift += 1

    def stack(x):
        return jnp.concatenate([jnp.where(first, x, 0.0), jnp.where(first, 0.0, x)], axis=0)

    masks = dict(strict=m_strict, incl=m_incl, eye=eye, pair=pair_mask, joins=join_masks)
    problems = []
    for pr in range(D_RWKV // LANE):
        sl = slice(pr * LANE, (pr + 1) * LANE)
        ztm, rtm, bhm, khm, bgm, kgm, vm = (_bf(stack(x[:, sl])) for x in (zt, rt, bh, kh, bg, kg, v))
        problems.append(dict(masks, ztm=ztm, rtm=rtm, bhm=bhm, khm=khm, bgm=bgm, kgm=kgm, vm=vm,
                             g_tot=g_tot[:, sl]))
    return problems


def _rwkv_solve(problems, sts):
    ps = range(len(problems))
    n2 = problems[0]['ztm'].shape[0]
    q = problems
    zr = [jnp.concatenate([q[i]['ztm'], q[i]['rtm']], axis=0) for i in ps]
    bk = [jnp.concatenate([q[i]['bhm'], q[i]['khm']], axis=0) for i in ps]
    g = [_mm_nt(zr[i], bk[i]) for i in ps]
    a_ab = [jnp.where(q[i]['strict'], g[i][:n2, :n2], 0.0) for i in ps]
    a_ak = [_bf(jnp.where(q[i]['strict'], g[i][:n2, n2:], 0.0)) for i in ps]
    a_rb = [_bf(jnp.where(q[i]['incl'], g[i][n2:, :n2], 0.0)) for i in ps]
    a_rk = [_bf(jnp.where(q[i]['incl'], g[i][n2:, n2:], 0.0)) for i in ps]
    sa_v = [_mm(a_ak[i], q[i]['vm']) for i in ps]
    tinv = [jnp.where(q[i]['eye'], 1.0, jnp.where(q[i]['pair'], a_ab[i], 0.0)) for i in ps]
    for level in range(len(q[0]['joins'])):
        tb16 = [_bf(tinv[i]) for i in ps]
        half = [_mm(tb16[i], jnp.where(q[i]['joins'][level], a_ab[i], 0.0)) for i in ps]
        tinv = [tinv[i] + _mm(half[i], tb16[i]) for i in ps]
    x = [_bf(_mm(tinv[i], jnp.concatenate([q[i]['ztm'], _bf(sa_v[i])], axis=1))) for i in ps]
    z_eff = [x[i][:, :LANE] for i in ps]
    bg_t = [q[i]['bgm'].T for i in ps]
    kg_t = [q[i]['kgm'].T for i in ps]
    r_eff = [_bf(q[i]['rtm'].astype(F32) + _mm(a_rb[i], z_eff[i])) for i in ps]
    p_mat = [_bf(jnp.where(q[i]['eye'], q[i]['g_tot'], 0.0) + _mm(bg_t[i], z_eff[i])) for i in ps]
    rhs = [jnp.concatenate([x[i][:, LANE:], q[i]['vm'], _bf(sts[i])], axis=0) for i in ps]
    lhs = [jnp.concatenate([jnp.concatenate([a_rb[i], a_rk[i], r_eff[i]], axis=1),
                            jnp.concatenate([bg_t[i], kg_t[i], p_mat[i]], axis=1)], axis=0) for i in ps]
    out = [_mm(lhs[i], rhs[i]) for i in ps]
    return [out[i][:n2] for i in ps], [out[i][n2:] for i in ps]


def _rwkv_scan_kernel(segs, cl, nb, rf_ref, vf_ref, kkf_ref, lwf_ref, kdf_ref, bf_ref,
                      rb_ref, vb_ref, kkb_ref, lwb_ref, kdb_ref, bb_ref, yf_ref, yb_ref, stf_ref, stb_ref):
    i = pl.program_id(0)
    pos_f, _ = _block_pos(i * cl, segs)
    pos_b, slen_b = _block_pos((nb - 1 - i) * cl, segs)

    @pl.when(pos_f == 0)
    def _():
        stf_ref[...] = jnp.zeros_like(stf_ref)

    @pl.when(pos_b + cl == slen_b)
    def _():
        stb_ref[...] = jnp.zeros_like(stb_ref)

    npair = D_RWKV // LANE
    problems = (_rwkv_chunk_problems(False, cl, rf_ref, vf_ref, kkf_ref, lwf_ref, kdf_ref, bf_ref)
                + _rwkv_chunk_problems(True, cl, rb_ref, vb_ref, kkb_ref, lwb_ref, kdb_ref, bb_ref))
    sts = [stf_ref[pr] for pr in range(npair)] + [stb_ref[pr] for pr in range(npair)]
    ym, st_new = _rwkv_solve(problems, sts)
    for pr in range(npair):
        sl = slice(pr * LANE, (pr + 1) * LANE)
        stf_ref[pr] = st_new[pr]
        stb_ref[pr] = st_new[npair + pr]
        yf_ref[:, sl] = ym[pr][:cl] + ym[pr][cl:]
        yb_ref[:, sl] = ym[npair + pr][:cl] + ym[npair + pr][cl:]


def _rwkv_post_kernel(yf_ref, yb_ref, bon_ref, g_ref, lng_ref, lnb_ref, hsel_ref, hselt_ref, o_ref):
    y = yf_ref[...] + yb_ref[...]
    hsel = hsel_ref[...]
    hselt = hselt_ref[...]
    mean = _head_sum(y, hsel, hselt) * (1.0 / HEAD_DIM)
    d = y - mean
    var = _head_sum(d * d, hsel, hselt) * (1.0 / HEAD_DIM)
    y = d * lax.rsqrt(var + RWKV_GN_EPS) * lng_ref[...] + lnb_ref[...]
    o_ref[...] = ((y + bon_ref[...]) * g_ref[...]).astype(o_ref.dtype)


def _rwkv_mixer(p, segs, mu, w_w, w_a, w0a0, g_up, k_k, k_a, r_k, ln_g, ln_b, hsel):
    hselt = hsel.T
    t = p.shape[0]
    c = D_RWKV
    tb = min(TOK_BLOCK, min(length for _, length in segs))
    prev_spec, next_spec = _halo_specs(tb, RWKV_COLS_PAD, t // SUBLANE)
    const = lambda shape: pl.BlockSpec(shape, lambda i: (0,) * len(shape))
    row_spec = lambda n, w: pl.BlockSpec((n, w), lambda i: (i, 0))
    vec = lambda a: a.reshape(1, c)
    tok = jax.ShapeDtypeStruct((t, c), F32)
    opd = jax.ShapeDtypeStruct((t, c), BF16)
    r, v, kk, lw0, lw1, kd0, kd1, b0, b1, g, bon = pl.pallas_call(
        functools.partial(_rwkv_pre_kernel, segs, tb),
        grid=(t // tb,),
        in_specs=[row_spec(tb, RWKV_COLS_PAD), prev_spec, next_spec, const((2, RWKV_COLS_PAD)),
                  const((RWKV_LO_WIN, 2 * c)), const((RWKV_LO_WIN, 2 * c)), const((1, 4 * c)),
                  const((RWKV_GATE_RANK, c)), const((1, c)), const((1, c)), const((1, c)),
                  const((c, LANE)), const((LANE, c))],
        out_specs=[row_spec(tb, c)] * 11,
        out_shape=[opd, opd, opd, tok, tok, opd, opd, opd, opd, tok, tok],
        compiler_params=pltpu.CompilerParams(
            dimension_semantics=("parallel",), vmem_limit_bytes=VMEM_LIMIT),
        name="rwkv_pre",
    )(p, p, p, mu, w_w, w_a, w0a0, g_up, vec(k_k), vec(k_a), vec(r_k), hsel, hselt)

    cl = RWKV_CHUNK
    nb = t // cl
    fwd_spec = pl.BlockSpec((cl, c), lambda i: (i, 0))
    bwd_spec = pl.BlockSpec((cl, c), lambda i: (nb - 1 - i, 0))
    state = pltpu.VMEM((c // LANE, LANE, LANE), F32)
    ys = pl.pallas_call(
        functools.partial(_rwkv_scan_kernel, segs, cl, nb),
        grid=(nb,),
        in_specs=[fwd_spec] * 6 + [bwd_spec] * 6,
        out_specs=[fwd_spec, bwd_spec],
        out_shape=[tok, tok],
        scratch_shapes=[state, state],
        compiler_params=pltpu.CompilerParams(
            dimension_semantics=("arbitrary",), vmem_limit_bytes=VMEM_LIMIT),
        name="rwkv_scan",
    )(r, v, kk, lw0, kd0, b0, r, v, kk, lw1, kd1, b1)

    return pl.pallas_call(
        _rwkv_post_kernel,
        grid=(t // tb,),
        in_specs=[row_spec(tb, c)] * 4 + [const((1, c)), const((1, c)), const((c, LANE)), const((LANE, c))],
        out_specs=row_spec(tb, c),
        out_shape=jax.ShapeDtypeStruct((t, c), MIX_OUT_DTYPE),
        compiler_params=pltpu.CompilerParams(
            dimension_semantics=("parallel",), vmem_limit_bytes=VMEM_LIMIT),
        name="rwkv_post",
    )(ys[0], ys[1], bon, g, vec(ln_g), vec(ln_b), hsel, hselt)


def _pad_cols(w, n):
    return jnp.pad(w, ((0, 0), (0, n - w.shape[1])))


def _lru_gate_blockdiag(gate_w):
    nh = gate_w.shape[2]
    eye = jnp.eye(nh, dtype=gate_w.dtype)
    w = jnp.einsum('dghij,hk->hidgkj', gate_w, eye)
    return w.reshape(D_LRU, 4 * D_LRU)


def _rwkv_lowrank_blocks(up, row0):
    out = jnp.zeros((RWKV_LO_WIN, 2 * D_RWKV), up.dtype)
    for d in range(2):
        out = lax.dynamic_update_slice(out, up[d], (row0 + d * RWKV_RANK, d * D_RWKV))
    return out


def _encoder(x, segs, w):
    (norm1_g, w_in, lru_conv_w, lru_conv_b, lru_gate_w, lru_gate_b, lru_lambda,
     rwkv_mu, rwkv_w0, rwkv_w_up, rwkv_a0, rwkv_a_up, rwkv_g_up, rwkv_k_k, rwkv_k_a,
     rwkv_r_k, rwkv_ln_g, rwkv_ln_b, ssd_conv_w, ssd_conv_b, ssd_dt_bias, ssd_a_log,
     ssd_d, ssd_norm_g, w_out, norm2_g, mlp_w1, mlp_w2, final_norm_g) = w
    depth = w_in.shape[0]
    head_id = jnp.arange(D_RWKV) // HEAD_DIM
    hsel = (head_id[:, None] == jnp.arange(LANE)[None, :]).astype(BF16)
    o1 = LRU_COLS
    o2 = LRU_COLS + RWKV_COLS
    pad_last = lambda a, n: jnp.pad(a, ((0, 0), (0, 0), (0, n - a.shape[2])))
    w_in_pad = jnp.concatenate([w_in[:, :, :o1], pad_last(w_in[:, :, o1:o2], RWKV_COLS_PAD),
                                pad_last(w_in[:, :, o2 + D_SSD:], SSD_Z_OFF), w_in[:, :, o2:o2 + D_SSD]],
                               axis=2).astype(BF16)
    for l in range(depth):
        p_lru, p_rwkv, p_ssd = _in_proj(x, norm1_g[l], w_in_pad[l])

        y_lru = _lru_mixer(p_lru, segs, lru_conv_w[l], lru_conv_b[l],
                           _lru_gate_blockdiag(lru_gate_w[l]).astype(BF16), lru_gate_b[l], lru_lambda[l])

        y_rwkv = _rwkv_mixer(
            p_rwkv, segs, _pad_cols(rwkv_mu[l], RWKV_COLS_PAD),
            _rwkv_lowrank_blocks(rwkv_w_up[l], 0).astype(BF16),
            _rwkv_lowrank_blocks(rwkv_a_up[l], RWKV_LO_WIN - 2 * RWKV_RANK).astype(BF16),
            jnp.concatenate([rwkv_w0[l, 0], rwkv_w0[l, 1], rwkv_a0[l, 0], rwkv_a0[l, 1]]).reshape(1, 4 * D_RWKV),
            rwkv_g_up[l].astype(BF16), rwkv_k_k[l], rwkv_k_a[l], rwkv_r_k[l], rwkv_ln_g[l], rwkv_ln_b[l], hsel)

        y_ssd = _ssd_mixer(
            p_ssd, segs, ssd_conv_w[l], ssd_conv_b[l],
            _pad_cols(ssd_dt_bias[l].reshape(1, 2 * SSD_HEADS), LANE),
            _pad_cols(ssd_a_log[l].reshape(1, 2 * SSD_HEADS), LANE),
            jnp.repeat(ssd_d[l], HEAD_DIM).reshape(1, D_SSD), ssd_norm_g[l])

        w_out_l = w_out[l].astype(BF16)
        x = _out_proj(x, y_lru, y_rwkv, y_ssd, w_out_l[:D_LRU], w_out_l[D_LRU:D_LRU + D_RWKV],
                      w_out_l[D_LRU + D_RWKV:])
        x = _mlp(x, norm2_g[l], mlp_w1[l].astype(BF16), mlp_w2[l].astype(BF16))
    return _final_norm(x, final_norm_g, segs[0][0] * segs[0][1])


def kernel(x_prompt, x_sample, norm1_g, w_in, lru_conv_w, lru_conv_b, lru_gate_w, lru_gate_b, lru_lambda, rwkv_mu, rwkv_w0, rwkv_w_up, rwkv_a0, rwkv_a_up, rwkv_g_up, rwkv_k_k, rwkv_k_a, rwkv_r_k, rwkv_ln_g, rwkv_ln_b, ssd_conv_w, ssd_conv_b, ssd_dt_bias, ssd_a_log, ssd_d, ssd_norm_g, w_out, norm2_g, mlp_w1, mlp_w2, final_norm_g):
    weights = (norm1_g, w_in, lru_conv_w, lru_conv_b, lru_gate_w, lru_gate_b, lru_lambda,
               rwkv_mu, rwkv_w0, rwkv_w_up, rwkv_a0, rwkv_a_up, rwkv_g_up, rwkv_k_k, rwkv_k_a,
               rwkv_r_k, rwkv_ln_g, rwkv_ln_b, ssd_conv_w, ssd_conv_b, ssd_dt_bias, ssd_a_log,
               ssd_d, ssd_norm_g, w_out, norm2_g, mlp_w1, mlp_w2, final_norm_g)
    bp, lp, d = x_prompt.shape
    bs, ls, _ = x_sample.shape
    segs = ((bp, lp), (bs, ls))
    x = jnp.concatenate([x_prompt.reshape(bp * lp, d), x_sample.reshape(bs * ls, d)], axis=0)
    y_prompt, y_sample = _encoder(x, segs, weights)
    return y_prompt.reshape(bp, lp, d), y_sample.reshape(bs, ls, d)
```

```python
import functools
import math

import jax
import jax.numpy as jnp
from jax import lax
from jax.experimental import pallas as pl
from jax.experimental.pallas import tpu as pltpu

F32 = jnp.float32
BF16 = jnp.bfloat16
MIX_OUT_DTYPE = BF16

D_MODEL = 2048
D_FF = 4 * D_MODEL
NORM_EPS = 1e-6
HEAD_DIM = 64
D_LRU = 512
LRU_C = 8.0
D_RWKV = 768
RWKV_RANK = 96
RWKV_GATE_RANK = 256
RWKV_LO_WIN = 256
RWKV_GN_EPS = 64e-5
RWKV_COLS = 3 * D_RWKV + 4 * RWKV_RANK + RWKV_GATE_RANK
RWKV_COLS_PAD = 3072
D_SSD = 768
SSD_HEADS = 12
SSD_GROUPS = 4
SSD_STATE = 128
SSD_BC = SSD_GROUPS * SSD_STATE
SSD_CONV_DIM = D_SSD + 2 * SSD_BC
SSD_COLS_PAD = 3072
SSD_DT_OFF = SSD_CONV_DIM
SSD_Z_OFF = SSD_COLS_PAD - D_SSD
LRU_COLS = 2 * D_LRU

LANE = 128
SUBLANE = 8
VMEM_LIMIT = 56 * 1024 * 1024

TOK_BLOCK = 256
LRU_BLOCK = 512
SSD_CHUNK = 128
SSD_DECAY_ROWS = 1024
SSD_FAC_COLS = 5 * LANE
RWKV_CHUNK = 64
MM_TM = 1024
IN_PROJ_TN = 512
MLP_TF = 512
FINAL_TM = 512


def _bf(x):
    return x.astype(BF16)


def _mm(a, b):
    return jnp.dot(_bf(a), _bf(b), preferred_element_type=F32)


def _mm_nt(a, b):
    return lax.dot_general(_bf(a), _bf(b), (((1,), (1,)), ((), ())), preferred_element_type=F32)


def _mm_tn(a, b):
    return lax.dot_general(_bf(a), _bf(b), (((0,), (0,)), ((), ())), preferred_element_type=F32)


def _split3(x):
    x1 = _bf(x)
    r1 = x - x1.astype(F32)
    x2 = _bf(r1)
    x3 = _bf(r1 - x2.astype(F32))
    return x1, x2, x3


def _mm_sel_lhs(sel, x):
    x1, x2, x3 = _split3(x)
    d = lambda v: jnp.dot(sel, v, preferred_element_type=F32)
    return d(x1) + d(x2) + d(x3)


def _mm_sel_rhs2(x, sel):
    x1 = _bf(x)
    x2 = _bf(x - x1.astype(F32))
    return jnp.dot(x1, sel, preferred_element_type=F32) + jnp.dot(x2, sel, preferred_element_type=F32)


def _head_sum(x, sel, sel_t):
    return _mm_sel_rhs2(_mm_sel_rhs2(x, sel), sel_t)


def _sigmoid(x):
    return 0.5 * jnp.tanh(0.5 * x) + 0.5


def _softplus(x):
    return jnp.maximum(x, 0.0) + jnp.log1p(jnp.exp(-jnp.abs(x)))


def _silu(x):
    return x * _sigmoid(x)


def _gelu_tanh(x):
    return 0.5 * x * (1.0 + jnp.tanh(math.sqrt(2.0 / math.pi) * (x + 0.044715 * (x * x * x))))


def _rms(x, g):
    return x * lax.rsqrt(jnp.mean(x * x, axis=-1, keepdims=True) + NORM_EPS) * g


def _block_pos(t0, segs):
    pos = None
    slen = None
    base = 0
    for n, length in segs:
        p = lax.rem(t0 - base, length)
        if pos is None:
            pos, slen = p, jnp.int32(length)
        else:
            inside = t0 >= base
            pos = jnp.where(inside, p, pos)
            slen = jnp.where(inside, jnp.int32(length), slen)
        base += n * length
    return pos, slen


def _total_tokens(segs):
    return sum(n * length for n, length in segs)


def _halo_specs(tb, width, nblocks_8, rev_nb=None):
    r = tb // SUBLANE
    if rev_nb is None:
        blk = lambda i: i
    else:
        blk = lambda i: rev_nb - 1 - i
    prev = pl.BlockSpec((SUBLANE, width), lambda i: (jnp.maximum(blk(i) * r - 1, 0), 0))
    nxt = pl.BlockSpec((SUBLANE, width), lambda i: (jnp.minimum((blk(i) + 1) * r, nblocks_8 - 1), 0))
    return prev, nxt


def _shifted(x, prev_row, next0, next1):
    n = x.shape[0]
    row = lax.broadcasted_iota(jnp.int32, (n, 1), 0)
    xm1 = jnp.where(row == 0, prev_row, pltpu.roll(x, 1, 0))
    xp1 = jnp.where(row == n - 1, next0, pltpu.roll(x, n - 1, 0))
    if next1 is None:
        return xm1, xp1, None
    xp2 = jnp.where(row == n - 2, next0, jnp.where(row == n - 1, next1, pltpu.roll(x, n - 2, 0)))
    return xm1, xp1, xp2


def _conv4(x, prev_row, next0, next1, w, b):
    xm1, xp1, xp2 = _shifted(x, prev_row, next0, next1)
    return w[0:1] * xm1 + w[1:2] * x + w[2:3] * xp1 + w[3:4] * xp2 + b


def _in_proj_kernel(n_lru, n_rwkv, x_ref, g_ref, w_ref, lru_ref, rwkv_ref, ssd_ref, u_ref):
    j = pl.program_id(1)

    @pl.when(j == 0)
    def _():
        u_ref[...] = _bf(_rms(x_ref[...], g_ref[...]))

    def project(o_ref):
        o_ref[...] = jnp.dot(u_ref[...], w_ref[...], preferred_element_type=F32)

    pl.when(j < n_lru)(functools.partial(project, lru_ref))
    pl.when((j >= n_lru) & (j < n_lru + n_rwkv))(functools.partial(project, rwkv_ref))
    pl.when(j >= n_lru + n_rwkv)(functools.partial(project, ssd_ref))


def _in_proj(x, g, w, layer):
    t, d = x.shape
    tm = min(MM_TM, t)
    tn = IN_PROJ_TN
    n_lru, n_rwkv, n_ssd = LRU_COLS // tn, RWKV_COLS_PAD // tn, SSD_COLS_PAD // tn
    return pl.pallas_call(
        functools.partial(_in_proj_kernel, n_lru, n_rwkv),
        grid=(t // tm, n_lru + n_rwkv + n_ssd),
        in_specs=[pl.BlockSpec((tm, d), lambda i, j: (i, 0)),
                  pl.BlockSpec((1, d), lambda i, j: (0, 0)),
                  pl.BlockSpec((None, d, tn), lambda i, j: (layer, 0, j))],
        out_specs=[pl.BlockSpec((tm, tn), lambda i, j: (i, jnp.minimum(j, n_lru - 1))),
                   pl.BlockSpec((tm, tn), lambda i, j: (i, jnp.clip(j - n_lru, 0, n_rwkv - 1))),
                   pl.BlockSpec((tm, tn), lambda i, j: (i, jnp.maximum(j - n_lru - n_rwkv, 0)))],
        out_shape=[jax.ShapeDtypeStruct((t, LRU_COLS), F32),
                   jax.ShapeDtypeStruct((t, RWKV_COLS_PAD), F32),
                   jax.ShapeDtypeStruct((t, SSD_COLS_PAD), F32)],
        scratch_shapes=[pltpu.VMEM((tm, d), BF16)],
        compiler_params=pltpu.CompilerParams(
            dimension_semantics=("arbitrary", "arbitrary"), vmem_limit_bytes=VMEM_LIMIT),
        name="in_proj",
    )(x, g.reshape(1, d), w)


def _out_proj_kernel(x_ref, ya_ref, yb_ref, yc_ref, wa_ref, wb_ref, wc_ref, o_ref):
    acc = jnp.dot(_bf(ya_ref[...]), wa_ref[...], preferred_element_type=F32)
    acc += jnp.dot(_bf(yb_ref[...]), wb_ref[...], preferred_element_type=F32)
    acc += jnp.dot(_bf(yc_ref[...]), wc_ref[...], preferred_element_type=F32)
    o_ref[...] = x_ref[...] + acc


def _out_proj(x, y_lru, y_rwkv, y_ssd, wa, wb, wc):
    t, d = x.shape
    tm = min(MM_TM, t)
    tn = 1024
    return pl.pallas_call(
        _out_proj_kernel,
        grid=(t // tm, d // tn),
        in_specs=[pl.BlockSpec((tm, tn), lambda i, j: (i, j)),
                  pl.BlockSpec((tm, D_LRU), lambda i, j: (i, 0)),
                  pl.BlockSpec((tm, D_RWKV), lambda i, j: (i, 0)),
                  pl.BlockSpec((tm, D_SSD), lambda i, j: (i, 0)),
                  pl.BlockSpec((D_LRU, tn), lambda i, j: (0, j)),
                  pl.BlockSpec((D_RWKV, tn), lambda i, j: (0, j)),
                  pl.BlockSpec((D_SSD, tn), lambda i, j: (0, j))],
        out_specs=pl.BlockSpec((tm, tn), lambda i, j: (i, j)),
        out_shape=jax.ShapeDtypeStruct((t, d), F32),
        compiler_params=pltpu.CompilerParams(
            dimension_semantics=("parallel", "arbitrary"), vmem_limit_bytes=VMEM_LIMIT),
        name="out_proj",
    )(x, y_lru, y_rwkv, y_ssd, wa, wb, wc)


def _mlp_kernel(x_ref, g_ref, w1_ref, w2_ref, o_ref, u_ref):
    @pl.when(pl.program_id(1) == 0)
    def _():
        x = x_ref[...]
        u_ref[...] = _bf(_rms(x, g_ref[...]))
        o_ref[...] = x

    h = jnp.dot(u_ref[...], w1_ref[...], preferred_element_type=F32)
    h = jnp.square(jnp.maximum(h, 0.0))
    o_ref[...] += jnp.dot(_bf(h), w2_ref[...], preferred_element_type=F32)


def _mlp(x, g, w1, w2, layer):
    t, d = x.shape
    ff = w1.shape[2]
    tm = min(MM_TM, t)
    tf = MLP_TF
    return pl.pallas_call(
        _mlp_kernel,
        grid=(t // tm, ff // tf),
        in_specs=[pl.BlockSpec((tm, d), lambda i, f: (i, 0)),
                  pl.BlockSpec((1, d), lambda i, f: (0, 0)),
                  pl.BlockSpec((None, d, tf), lambda i, f: (layer, 0, f)),
                  pl.BlockSpec((None, tf, d), lambda i, f: (layer, f, 0))],
        out_specs=pl.BlockSpec((tm, d), lambda i, f: (i, 0)),
        out_shape=jax.ShapeDtypeStruct((t, d), F32),
        scratch_shapes=[pltpu.VMEM((tm, d), BF16)],
        compiler_params=pltpu.CompilerParams(
            dimension_semantics=("parallel", "arbitrary"), vmem_limit_bytes=VMEM_LIMIT),
        name="mlp",
    )(x, g.reshape(1, d), w1, w2)


def _final_norm_kernel(na, x_ref, g_ref, oa_ref, ob_ref):
    y = _rms(x_ref[...], g_ref[...])
    i = pl.program_id(0)

    @pl.when(i < na)
    def _():
        oa_ref[...] = y

    @pl.when(i >= na)
    def _():
        ob_ref[...] = y


def _final_norm(x, g, ta):
    t, d = x.shape
    tm = min(FINAL_TM, ta, t - ta)
    na = ta // tm
    return pl.pallas_call(
        functools.partial(_final_norm_kernel, na),
        grid=(t // tm,),
        in_specs=[pl.BlockSpec((tm, d), lambda i: (i, 0)),
                  pl.BlockSpec((1, d), lambda i: (0, 0))],
        out_specs=[pl.BlockSpec((tm, d), lambda i: (jnp.minimum(i, na - 1), 0)),
                   pl.BlockSpec((tm, d), lambda i: (jnp.maximum(i - na, 0), 0))],
        out_shape=[jax.ShapeDtypeStruct((ta, d), F32), jax.ShapeDtypeStruct((t - ta, d), F32)],
        compiler_params=pltpu.CompilerParams(dimension_semantics=("arbitrary",)),
        name="final_norm",
    )(x, g.reshape(1, d))


def _scan_rows(a, b, reverse):
    n = a.shape[0]
    row = lax.broadcasted_iota(jnp.int32, (n, 1), 0)
    s = 1
    while s < n:
        shift = n - s if reverse else s
        valid = (row < n - s) if reverse else (row >= s)
        a_sh = pltpu.roll(a, shift, 0)
        b_sh = pltpu.roll(b, shift, 0)
        b = jnp.where(valid, a * b_sh, 0.0) + b
        a = jnp.where(valid, a * a_sh, a)
        s *= 2
    return a, b


def _scan_block(a, b, carry, reverse):
    tiles = range(a.shape[0] // SUBLANE)
    loc = [_scan_rows(a[k * SUBLANE:(k + 1) * SUBLANE], b[k * SUBLANE:(k + 1) * SUBLANE], reverse) for k in tiles]
    hs = [None] * len(tiles)
    for k in (reversed(tiles) if reverse else tiles):
        a_cum, h_loc = loc[k]
        h = h_loc + a_cum * carry
        carry = h[0:1] if reverse else h[SUBLANE - 1:SUBLANE]
        hs[k] = h
    return jnp.concatenate(hs, axis=0), carry


def _lru_fwd_kernel(segs, tb, p_ref, pprev_ref, pnext_ref, cw_ref, cb_ref, wbd_ref, gb_ref, lam_ref,
                    hf_ref, a1_ref, bx1_ref, carry_ref):
    i = pl.program_id(0)
    pos, slen = _block_pos(i * tb, segs)
    is_start = pos == 0
    is_end = pos + tb == slen
    xb = p_ref[:, D_LRU:]
    prev_row = jnp.where(is_start, 0.0, pprev_ref[SUBLANE - 1:SUBLANE, D_LRU:])
    next0 = jnp.where(is_end, 0.0, pnext_ref[0:1, D_LRU:])
    next1 = jnp.where(is_end, 0.0, pnext_ref[1:2, D_LRU:])
    xc = _conv4(xb, prev_row, next0, next1, cw_ref[...], cb_ref[...])
    gates = _sigmoid(_mm(xc, wbd_ref[...]) + gb_ref[...])
    sp = _softplus(-lam_ref[...])

    def direction(d):
        r = gates[:, (2 * d) * D_LRU:(2 * d + 1) * D_LRU]
        inp = gates[:, (2 * d + 1) * D_LRU:(2 * d + 2) * D_LRU]
        log_a = -LRU_C * r * sp[d:d + 1]
        th = jnp.tanh(log_a)
        one_minus_a2 = -2.0 * th / (1.0 - th)
        return jnp.exp(log_a), jnp.sqrt(one_minus_a2) * (inp * xc)

    a0, bx0 = direction(0)
    a1, bx1 = direction(1)
    a1_ref[...] = a1
    bx1_ref[...] = bx1

    @pl.when(is_start)
    def _():
        carry_ref[...] = jnp.zeros_like(carry_ref)

    h, carry = _scan_block(a0, bx0, carry_ref[...], False)
    hf_ref[...] = h
    carry_ref[...] = carry


def _lru_bwd_kernel(segs, tb, nb, p_ref, a1_ref, bx1_ref, hf_ref, o_ref, carry_ref):
    blk = nb - 1 - pl.program_id(0)
    pos, slen = _block_pos(blk * tb, segs)

    @pl.when(pos + tb == slen)
    def _():
        carry_ref[...] = jnp.zeros_like(carry_ref)

    h, carry = _scan_block(a1_ref[...], bx1_ref[...], carry_ref[...], True)
    carry_ref[...] = carry
    o_ref[...] = ((hf_ref[...] + h) * _gelu_tanh(p_ref[:, :D_LRU])).astype(o_ref.dtype)


def _lru_mixer(p, segs, conv_w, conv_b, w_bd, gate_b, lam):
    t = p.shape[0]
    tb = min(LRU_BLOCK, min(length for _, length in segs))
    nb = t // tb
    prev_spec, next_spec = _halo_specs(tb, LRU_COLS, t // SUBLANE)
    const = lambda shape: pl.BlockSpec(shape, lambda i: (0,) * len(shape))
    row_spec = lambda w: pl.BlockSpec((tb, w), lambda i: (i, 0))
    hf, a1, bx1 = pl.pallas_call(
        functools.partial(_lru_fwd_kernel, segs, tb),
        grid=(nb,),
        in_specs=[row_spec(LRU_COLS), prev_spec, next_spec, const((4, D_LRU)), const((1, D_LRU)),
                  const((D_LRU, 4 * D_LRU)), const((1, 4 * D_LRU)), const((2, D_LRU))],
        out_specs=[row_spec(D_LRU)] * 3,
        out_shape=[jax.ShapeDtypeStruct((t, D_LRU), F32)] * 3,
        scratch_shapes=[pltpu.VMEM((1, D_LRU), F32)],
        compiler_params=pltpu.CompilerParams(
            dimension_semantics=("arbitrary",), vmem_limit_bytes=VMEM_LIMIT),
        name="lru_fwd",
    )(p, p, p, conv_w, conv_b.reshape(1, D_LRU), w_bd, gate_b.reshape(1, 4 * D_LRU), lam)
    rev_spec = lambda w: pl.BlockSpec((tb, w), lambda i: (nb - 1 - i, 0))
    return pl.pallas_call(
        functools.partial(_lru_bwd_kernel, segs, tb, nb),
        grid=(nb,),
        in_specs=[rev_spec(LRU_COLS), rev_spec(D_LRU), rev_spec(D_LRU), rev_spec(D_LRU)],
        out_specs=rev_spec(D_LRU),
        out_shape=jax.ShapeDtypeStruct((t, D_LRU), MIX_OUT_DTYPE),
        scratch_shapes=[pltpu.VMEM((1, D_LRU), F32)],
        compiler_params=pltpu.CompilerParams(
            dimension_semantics=("arbitrary",), vmem_limit_bytes=VMEM_LIMIT),
        name="lru_bwd",
    )(p, a1, bx1, hf)


def _ssd_core(reverse, xs, bm, cm, fac, u_t, hexp, ht_ref):
    q = xs.shape[0]
    off = SSD_HEADS if reverse else 0
    row = lax.broadcasted_iota(jnp.int32, (q, q), 0)
    col = lax.broadcasted_iota(jnp.int32, (q, q), 1)
    mask = (col >= row) if reverse else (col <= row)
    dt, u, out_fac, in_fac = (fac[:, n * LANE:(n + 1) * LANE] for n in range(4))
    chunk_decay = fac[0:1, 4 * LANE:5 * LANE]
    lane = lax.broadcasted_iota(jnp.int32, (1, LANE), 1)
    first = lane < HEAD_DIM
    heads = range(SSD_HEADS)
    pairs = range(SSD_HEADS // 2)
    grp = [h // (SSD_HEADS // SSD_GROUPS) for h in heads]
    cols = [slice(off + h, off + h + 1) for h in heads]
    gsl = [slice(g * SSD_STATE, (g + 1) * SSD_STATE) for g in range(SSD_GROUPS)]
    cm_g = [_bf(cm[:, sl]) for sl in gsl]
    bm_g = [_bf(bm[:, sl]) for sl in gsl]
    cb = [_mm_nt(cm_g[g], bm_g[g]) for g in range(SSD_GROUPS)]
    hts = [ht_ref[pr] for pr in pairs]
    hts_bf = [_bf(ht) for ht in hts]
    x_pair = [xs[:, pr * LANE:(pr + 1) * LANE] for pr in pairs]
    seg = [(u_t[c, :] - u[:, c]) if reverse else (u[:, c] - u_t[c, :]) for c in cols]
    m = [_bf(cb[grp[h]] * jnp.exp(jnp.where(mask, seg[h], -1e30))) for h in heads]
    dt_x = _mm_sel_rhs2(dt, hexp)
    dt_in_x = _mm_sel_rhs2(dt * in_fac, hexp)
    out_x = _mm_sel_rhs2(out_fac, hexp)
    psl = [slice(pr * LANE, (pr + 1) * LANE) for pr in pairs]
    xd = [_bf(x_pair[pr] * dt_x[:, psl[pr]]) for pr in pairs]
    xd_in = [_bf(x_pair[pr] * dt_in_x[:, psl[pr]]) for pr in pairs]
    y_diag = [_mm(m[h], xd[h // 2]) for h in heads]
    c_ht = {key: _mm(cm_g[key[1]], hts_bf[key[0]]) for key in sorted({(h // 2, grp[h]) for h in heads})}
    h_new = [chunk_decay[:, cols[h]] * hts[h // 2] + _mm_tn(bm_g[grp[h]], xd_in[h // 2]) for h in heads]
    for pr in pairs:
        ht_ref[pr] = jnp.where(first, h_new[2 * pr], h_new[2 * pr + 1])
    pick = lambda pr, vals: jnp.where(first, vals[0], vals[1])
    return [pick(pr, (y_diag[2 * pr], y_diag[2 * pr + 1]))
            + pick(pr, (c_ht[(pr, grp[2 * pr])], c_ht[(pr, grp[2 * pr + 1])])) * out_x[:, psl[pr]] for pr in pairs]


def _ssd_decay_kernel(q, dtraw_ref, dtb_ref, alog_ref, fac_ref, ut_ref):
    ks = range(dtraw_ref.shape[0] // q)
    rows = [slice(k * q, (k + 1) * q) for k in ks]
    lane = lax.broadcasted_iota(jnp.int32, (1, LANE), 1)
    fwd_lane = lane < SSD_HEADS
    a_head = jnp.where(lane < 2 * SSD_HEADS, -jnp.exp(alog_ref[...]), 0.0)
    row = lax.broadcasted_iota(jnp.int32, (q, q), 0)
    col = lax.broadcasted_iota(jnp.int32, (q, q), 1)
    lower = _bf((col <= row).astype(F32))
    dt = [_softplus(dtraw_ref[r, :] + dtb_ref[...]) for r in rows]
    da = [dt[k] * a_head for k in ks]
    cum = [_mm_sel_lhs(lower, da[k]) for k in ks]
    total = [cum[k][q - 1:q] for k in ks]
    u = [jnp.where(fwd_lane, cum[k], cum[k] - da[k]) for k in ks]
    e_u = [jnp.exp(u[k]) for k in ks]
    e_rest = [jnp.exp(total[k] - u[k]) for k in ks]
    for k in ks:
        pieces = (dt[k], u[k], jnp.where(fwd_lane, e_u[k], e_rest[k]), jnp.where(fwd_lane, e_rest[k], e_u[k]),
                  jnp.broadcast_to(jnp.exp(total[k]), (q, LANE)))
        for n, piece in enumerate(pieces):
            fac_ref[rows[k], n * LANE:(n + 1) * LANE] = piece
        ut_ref[:, rows[k]] = u[k].T


def _ssd_fwd_kernel(segs, q, p_ref, pprev_ref, pnext_ref, cw_ref, cb_ref, fac_ref, ut_ref, hexp_ref, dskip_ref,
                    xconv_ref, yacc_ref, ht_ref):
    i = pl.program_id(0)
    pos, slen = _block_pos(i * q, segs)
    is_start = pos == 0
    is_end = pos + q == slen
    prev_row = jnp.where(is_start, 0.0, pprev_ref[SUBLANE - 1:SUBLANE, :])
    next0 = jnp.where(is_end, 0.0, pnext_ref[0:1, :])
    next1 = jnp.where(is_end, 0.0, pnext_ref[1:2, :])
    xbc = _silu(_conv4(p_ref[...], prev_row, next0, next1, cw_ref[...], cb_ref[...]))
    xconv_ref[...] = xbc
    xs = xbc[:, :D_SSD]

    @pl.when(is_start)
    def _():
        ht_ref[...] = jnp.zeros_like(ht_ref)

    ys = _ssd_core(False, xs, xbc[:, D_SSD:D_SSD + SSD_BC], xbc[:, D_SSD + SSD_BC:], fac_ref[...], ut_ref[...],
                   hexp_ref[...], ht_ref)
    for pr, y in enumerate(ys):
        sl = slice(pr * LANE, (pr + 1) * LANE)
        yacc_ref[:, sl] = y + dskip_ref[:, sl] * xs[:, sl]


def _ssd_bwd_kernel(segs, q, nb, z_ref, xconv_ref, yacc_ref, fac_ref, ut_ref, hexp_ref, ng_ref, o_ref, ht_ref):
    blk = nb - 1 - pl.program_id(0)
    pos, slen = _block_pos(blk * q, segs)

    @pl.when(pos + q == slen)
    def _():
        ht_ref[...] = jnp.zeros_like(ht_ref)

    xbc = xconv_ref[...]
    ys = _ssd_core(True, xbc[:, :D_SSD], xbc[:, D_SSD:D_SSD + SSD_BC], xbc[:, D_SSD + SSD_BC:], fac_ref[...],
                   ut_ref[...], hexp_ref[...], ht_ref)
    y = jnp.concatenate(ys, axis=1) + yacc_ref[...]
    o_ref[...] = _rms(y * _silu(z_ref[...]), ng_ref[...]).astype(o_ref.dtype)


def _ssd_mixer(p, segs, conv_w, conv_b, dt_bias, a_log, d_skip, norm_g):
    t = p.shape[0]
    q = SSD_CHUNK
    nb = t // q
    prev_spec, next_spec = _halo_specs(q, SSD_CONV_DIM, t // SUBLANE)
    const = lambda shape: pl.BlockSpec(shape, lambda i: (0,) * len(shape))
    row_spec = lambda w: pl.BlockSpec((q, w), lambda i: (i, 0))
    state = pltpu.VMEM((SSD_HEADS // 2, SSD_STATE, LANE), F32)
    rows = min(SSD_DECAY_ROWS, t)
    dt_col = SSD_DT_OFF // LANE
    head_expand = lambda off: (jnp.arange(LANE)[:, None] == off + jnp.arange(D_SSD)[None, :] // HEAD_DIM).astype(BF16)
    fac, u_t = pl.pallas_call(
        functools.partial(_ssd_decay_kernel, q),
        grid=(t // rows,),
        in_specs=[pl.BlockSpec((rows, LANE), lambda i: (i, dt_col)), const((1, LANE)), const((1, LANE))],
        out_specs=[pl.BlockSpec((rows, SSD_FAC_COLS), lambda i: (i, 0)), pl.BlockSpec((LANE, rows), lambda i: (0, i))],
        out_shape=[jax.ShapeDtypeStruct((t, SSD_FAC_COLS), F32), jax.ShapeDtypeStruct((LANE, t), F32)],
        compiler_params=pltpu.CompilerParams(
            dimension_semantics=("parallel",), vmem_limit_bytes=VMEM_LIMIT),
        name="ssd_decay",
    )(p, dt_bias, a_log)
    xconv, yacc = pl.pallas_call(
        functools.partial(_ssd_fwd_kernel, segs, q),
        grid=(nb,),
        in_specs=[row_spec(SSD_CONV_DIM), prev_spec, next_spec, const((4, SSD_CONV_DIM)),
                  const((1, SSD_CONV_DIM)), row_spec(SSD_FAC_COLS), pl.BlockSpec((LANE, q), lambda i: (0, i)),
                  const((LANE, D_SSD)), const((1, D_SSD))],
        out_specs=[row_spec(SSD_CONV_DIM), row_spec(D_SSD)],
        out_shape=[jax.ShapeDtypeStruct((t, SSD_CONV_DIM), F32), jax.ShapeDtypeStruct((t, D_SSD), F32)],
        scratch_shapes=[state],
        compiler_params=pltpu.CompilerParams(
            dimension_semantics=("arbitrary",), vmem_limit_bytes=VMEM_LIMIT),
        name="ssd_fwd",
    )(p, p, p, conv_w, conv_b.reshape(1, SSD_CONV_DIM), fac, u_t, head_expand(0), d_skip)
    rev_spec = lambda w: pl.BlockSpec((q, w), lambda i: (nb - 1 - i, 0))
    return pl.pallas_call(
        functools.partial(_ssd_bwd_kernel, segs, q, nb),
        grid=(nb,),
        in_specs=[pl.BlockSpec((q, D_SSD), lambda i: (nb - 1 - i, SSD_Z_OFF // D_SSD)),
                  rev_spec(SSD_CONV_DIM), rev_spec(D_SSD), rev_spec(SSD_FAC_COLS),
                  pl.BlockSpec((LANE, q), lambda i: (0, nb - 1 - i)), const((LANE, D_SSD)), const((1, D_SSD))],
        out_specs=rev_spec(D_SSD),
        out_shape=jax.ShapeDtypeStruct((t, D_SSD), MIX_OUT_DTYPE),
        scratch_shapes=[state],
        compiler_params=pltpu.CompilerParams(
            dimension_semantics=("arbitrary",), vmem_limit_bytes=VMEM_LIMIT),
        name="ssd_bwd",
    )(p, xconv, yacc, fac, u_t, head_expand(SSD_HEADS), norm_g.reshape(1, D_SSD))


def _rwkv_pre_kernel(segs, tb, p_ref, pprev_ref, pnext_ref, mu_ref, ww_ref, wa_ref, w0a0_ref, gup_ref, kk_ref_w,
                     ka_ref, rk_ref, hsel_ref, hselt_ref,
                     r_out, v_out, kk_out, lw0_out, lw1_out, kd0_out, kd1_out, b0_out, b1_out, g_out, bon_out):
    i = pl.program_id(0)
    pos, slen = _block_pos(i * tb, segs)
    p = p_ref[...]
    prev_row = jnp.where(pos == 0, 0.0, pprev_ref[SUBLANE - 1:SUBLANE, :])
    next_row = jnp.where(pos + tb == slen, 0.0, pnext_ref[0:1, :])
    pm1, pp1, _ = _shifted(p, prev_row, next_row, None)
    mu = mu_ref[...]
    ps = p + mu[0:1] * (pm1 - p) + mu[1:2] * (pp1 - p)
    c = D_RWKV
    r = ps[:, 0:c]
    k = ps[:, c:2 * c]
    v = ps[:, 2 * c:3 * c]
    lo0 = 3 * c
    w_win = ps[:, lo0:lo0 + RWKV_LO_WIN]
    a_win = ps[:, lo0 + 4 * RWKV_RANK - RWKV_LO_WIN:lo0 + 4 * RWKV_RANK]
    g_lo = ps[:, lo0 + 4 * RWKV_RANK:lo0 + 4 * RWKV_RANK + RWKV_GATE_RANK]
    pre_w = _mm(jnp.tanh(w_win), ww_ref[...]) + w0a0_ref[:, :2 * c]
    pre_a = _mm(a_win, wa_ref[...]) + w0a0_ref[:, 2 * c:]
    g = _mm(_sigmoid(g_lo), gup_ref[...])
    hsel = hsel_ref[...]
    hselt = hselt_ref[...]
    kk = k * kk_ref_w[...]
    ss = _head_sum(kk * kk, hsel, hselt)
    kk = kk / jnp.maximum(jnp.sqrt(ss), 1e-12)
    kd_sum = None
    outs = ((lw0_out, kd0_out, b0_out), (lw1_out, kd1_out, b1_out))
    for d in range(2):
        w_pre = pre_w[:, d * c:(d + 1) * c]
        a = _sigmoid(pre_a[:, d * c:(d + 1) * c])
        outs[d][0][...] = -math.exp(-0.5) * _sigmoid(w_pre)
        kd = k * (1.0 + (a - 1.0) * ka_ref[...])
        outs[d][1][...] = kd.astype(outs[d][1].dtype)
        outs[d][2][...] = (kk * a).astype(outs[d][2].dtype)
        kd_sum = kd if kd_sum is None else kd_sum + kd
    r_out[...] = r.astype(r_out.dtype)
    v_out[...] = v.astype(v_out.dtype)
    kk_out[...] = kk.astype(kk_out.dtype)
    g_out[...] = g
    bon_out[...] = _head_sum(r * kd_sum * rk_ref[...], hsel, hselt) * v


def _rwkv_chunk_problems(reverse, cl, r_ref, v_ref, kk_ref, lw_ref, kd_ref, b_ref):
    n2 = 2 * cl
    rowc = lax.broadcasted_iota(jnp.int32, (cl, cl), 0)
    colc = lax.broadcasted_iota(jnp.int32, (cl, cl), 1)
    tri = _bf(((colc >= rowc) if reverse else (colc <= rowc)).astype(F32))
    lw = lw_ref[...]
    cum = _mm_sel_lhs(tri, lw)
    total = cum[0:1] if reverse else cum[cl - 1:cl]
    g_in = jnp.exp(cum)
    g_prev = jnp.exp(cum - lw)
    g_inv = jnp.exp(-cum)
    g_end = jnp.exp(total - cum)
    g_tot = jnp.exp(total)
    kk = kk_ref[...]
    zt = -kk * g_prev
    rt = r_ref[...] * g_in
    bh = b_ref[...] * g_inv
    kh = kd_ref[...] * g_inv
    bg = b_ref[...] * g_end
    kg = kd_ref[...] * g_end
    v = v_ref[...]

    row = lax.broadcasted_iota(jnp.int32, (n2, n2), 0)
    col = lax.broadcasted_iota(jnp.int32, (n2, n2), 1)
    same = (row >= cl) == (col >= cl)
    rt_i = jnp.where(row >= cl, row - cl, row)
    ct_i = jnp.where(col >= cl, col - cl, col)
    if reverse:
        m_strict = same & (ct_i > rt_i)
        m_incl = same & (ct_i >= rt_i)
    else:
        m_strict = same & (ct_i < rt_i)
        m_incl = same & (ct_i <= rt_i)
    eye = row == col
    first = lax.broadcasted_iota(jnp.int32, (1, LANE), 1) < HEAD_DIM
    pair_mask = same & ((rt_i >> 1) == (ct_i >> 1))
    join_masks = []
    shift = 1
    while (1 << shift) < cl:
        later, earlier = (ct_i, rt_i) if reverse else (rt_i, ct_i)
        join_masks.append(same & ((rt_i >> (shift + 1)) == (ct_i >> (shift + 1)))
                          & (((later >> shift) & 1) == 1) & (((earlier >> shift) & 1) == 0))
        shift += 1

    def stack(x):
        return jnp.concatenate([jnp.where(first, x, 0.0), jnp.where(first, 0.0, x)], axis=0)

    masks = dict(strict=m_strict, incl=m_incl, eye=eye, pair=pair_mask, joins=join_masks)
    problems = []
    for pr in range(D_RWKV // LANE):
        sl = slice(pr * LANE, (pr + 1) * LANE)
        ztm, rtm, bhm, khm, bgm, kgm, vm = (_bf(stack(x[:, sl])) for x in (zt, rt, bh, kh, bg, kg, v))
        problems.append(dict(masks, ztm=ztm, rtm=rtm, bhm=bhm, khm=khm, bgm=bgm, kgm=kgm, vm=vm,
                             g_tot=g_tot[:, sl]))
    return problems


def _rwkv_solve(problems, sts):
    ps = range(len(problems))
    n2 = problems[0]['ztm'].shape[0]
    q = problems
    zr = [jnp.concatenate([q[i]['ztm'], q[i]['rtm']], axis=0) for i in ps]
    bk = [jnp.concatenate([q[i]['bhm'], q[i]['khm']], axis=0) for i in ps]
    g = [_mm_nt(zr[i], bk[i]) for i in ps]
    a_ab = [jnp.where(q[i]['strict'], g[i][:n2, :n2], 0.0) for i in ps]
    a_ak = [_bf(jnp.where(q[i]['strict'], g[i][:n2, n2:], 0.0)) for i in ps]
    a_rb = [_bf(jnp.where(q[i]['incl'], g[i][n2:, :n2], 0.0)) for i in ps]
    a_rk = [_bf(jnp.where(q[i]['incl'], g[i][n2:, n2:], 0.0)) for i in ps]
    sa_v = [_mm(a_ak[i], q[i]['vm']) for i in ps]
    tinv = [jnp.where(q[i]['eye'], 1.0, jnp.where(q[i]['pair'], a_ab[i], 0.0)) for i in ps]
    for level in range(len(q[0]['joins'])):
        tb16 = [_bf(tinv[i]) for i in ps]
        half = [_mm(tb16[i], jnp.where(q[i]['joins'][level], a_ab[i], 0.0)) for i in ps]
        tinv = [tinv[i] + _mm(half[i], tb16[i]) for i in ps]
    x = [_bf(_mm(tinv[i], jnp.concatenate([q[i]['ztm'], _bf(sa_v[i])], axis=1))) for i in ps]
    z_eff = [x[i][:, :LANE] for i in ps]
    bg_t = [q[i]['bgm'].T for i in ps]
    kg_t = [q[i]['kgm'].T for i in ps]
    r_eff = [_bf(q[i]['rtm'].astype(F32) + _mm(a_rb[i], z_eff[i])) for i in ps]
    p_mat = [_bf(jnp.where(q[i]['eye'], q[i]['g_tot'], 0.0) + _mm(bg_t[i], z_eff[i])) for i in ps]
    rhs = [jnp.concatenate([x[i][:, LANE:], q[i]['vm'], _bf(sts[i])], axis=0) for i in ps]
    lhs = [jnp.concatenate([jnp.concatenate([a_rb[i], a_rk[i], r_eff[i]], axis=1),
                            jnp.concatenate([bg_t[i], kg_t[i], p_mat[i]], axis=1)], axis=0) for i in ps]
    out = [_mm(lhs[i], rhs[i]) for i in ps]
    return [out[i][:n2] for i in ps], [out[i][n2:] for i in ps]


def _rwkv_scan_kernel(segs, cl, nb, rf_ref, vf_ref, kkf_ref, lwf_ref, kdf_ref, bf_ref,
                      rb_ref, vb_ref, kkb_ref, lwb_ref, kdb_ref, bb_ref, yf_ref, yb_ref, stf_ref, stb_ref):
    i = pl.program_id(0)
    pos_f, _ = _block_pos(i * cl, segs)
    pos_b, slen_b = _block_pos((nb - 1 - i) * cl, segs)

    @pl.when(pos_f == 0)
    def _():
        stf_ref[...] = jnp.zeros_like(stf_ref)

    @pl.when(pos_b + cl == slen_b)
    def _():
        stb_ref[...] = jnp.zeros_like(stb_ref)

    npair = D_RWKV // LANE
    problems = (_rwkv_chunk_problems(False, cl, rf_ref, vf_ref, kkf_ref, lwf_ref, kdf_ref, bf_ref)
                + _rwkv_chunk_problems(True, cl, rb_ref, vb_ref, kkb_ref, lwb_ref, kdb_ref, bb_ref))
    sts = [stf_ref[pr] for pr in range(npair)] + [stb_ref[pr] for pr in range(npair)]
    ym, st_new = _rwkv_solve(problems, sts)
    for pr in range(npair):
        sl = slice(pr * LANE, (pr + 1) * LANE)
        stf_ref[pr] = st_new[pr]
        stb_ref[pr] = st_new[npair + pr]
        yf_ref[:, sl] = ym[pr][:cl] + ym[pr][cl:]
        yb_ref[:, sl] = ym[npair + pr][:cl] + ym[npair + pr][cl:]


def _rwkv_post_kernel(yf_ref, yb_ref, bon_ref, g_ref, lng_ref, lnb_ref, hsel_ref, hselt_ref, o_ref):
    y = yf_ref[...] + yb_ref[...]
    hsel = hsel_ref[...]
    hselt = hselt_ref[...]
    mean = _head_sum(y, hsel, hselt) * (1.0 / HEAD_DIM)
    d = y - mean
    var = _head_sum(d * d, hsel, hselt) * (1.0 / HEAD_DIM)
    y = d * lax.rsqrt(var + RWKV_GN_EPS) * lng_ref[...] + lnb_ref[...]
    o_ref[...] = ((y + bon_ref[...]) * g_ref[...]).astype(o_ref.dtype)


def _rwkv_mixer(p, segs, mu, w_w, w_a, w0a0, g_up, k_k, k_a, r_k, ln_g, ln_b, hsel):
    hselt = hsel.T
    t = p.shape[0]
    c = D_RWKV
    tb = min(TOK_BLOCK, min(length for _, length in segs))
    prev_spec, next_spec = _halo_specs(tb, RWKV_COLS_PAD, t // SUBLANE)
    const = lambda shape: pl.BlockSpec(shape, lambda i: (0,) * len(shape))
    row_spec = lambda n, w: pl.BlockSpec((n, w), lambda i: (i, 0))
    vec = lambda a: a.reshape(1, c)
    tok = jax.ShapeDtypeStruct((t, c), F32)
    opd = jax.ShapeDtypeStruct((t, c), BF16)
    r, v, kk, lw0, lw1, kd0, kd1, b0, b1, g, bon = pl.pallas_call(
        functools.partial(_rwkv_pre_kernel, segs, tb),
        grid=(t // tb,),
        in_specs=[row_spec(tb, RWKV_COLS_PAD), prev_spec, next_spec, const((2, RWKV_COLS_PAD)),
                  const((RWKV_LO_WIN, 2 * c)), const((RWKV_LO_WIN, 2 * c)), const((1, 4 * c)),
                  const((RWKV_GATE_RANK, c)), const((1, c)), const((1, c)), const((1, c)),
                  const((c, LANE)), const((LANE, c))],
        out_specs=[row_spec(tb, c)] * 11,
        out_shape=[opd, opd, opd, tok, tok, opd, opd, opd, opd, tok, tok],
        compiler_params=pltpu.CompilerParams(
            dimension_semantics=("parallel",), vmem_limit_bytes=VMEM_LIMIT),
        name="rwkv_pre",
    )(p, p, p, mu, w_w, w_a, w0a0, g_up, vec(k_k), vec(k_a), vec(r_k), hsel, hselt)

    cl = RWKV_CHUNK
    nb = t // cl
    fwd_spec = pl.BlockSpec((cl, c), lambda i: (i, 0))
    bwd_spec = pl.BlockSpec((cl, c), lambda i: (nb - 1 - i, 0))
    state = pltpu.VMEM((c // LANE, LANE, LANE), F32)
    ys = pl.pallas_call(
        functools.partial(_rwkv_scan_kernel, segs, cl, nb),
        grid=(nb,),
        in_specs=[fwd_spec] * 6 + [bwd_spec] * 6,
        out_specs=[fwd_spec, bwd_spec],
        out_shape=[tok, tok],
        scratch_shapes=[state, state],
        compiler_params=pltpu.CompilerParams(
            dimension_semantics=("arbitrary",), vmem_limit_bytes=VMEM_LIMIT),
        name="rwkv_scan",
    )(r, v, kk, lw0, kd0, b0, r, v, kk, lw1, kd1, b1)

    return pl.pallas_call(
        _rwkv_post_kernel,
        grid=(t // tb,),
        in_specs=[row_spec(tb, c)] * 4 + [const((1, c)), const((1, c)), const((c, LANE)), const((LANE, c))],
        out_specs=row_spec(tb, c),
        out_shape=jax.ShapeDtypeStruct((t, c), MIX_OUT_DTYPE),
        compiler_params=pltpu.CompilerParams(
            dimension_semantics=("parallel",), vmem_limit_bytes=VMEM_LIMIT),
        name="rwkv_post",
    )(ys[0], ys[1], bon, g, vec(ln_g), vec(ln_b), hsel, hselt)


def _pad_cols(w, n):
    return jnp.pad(w, ((0, 0), (0, n - w.shape[1])))


def _lru_gate_blockdiag(gate_w):
    nh = gate_w.shape[2]
    eye = jnp.eye(nh, dtype=gate_w.dtype)
    w = jnp.einsum('dghij,hk->hidgkj', gate_w, eye)
    return w.reshape(D_LRU, 4 * D_LRU)


def _rwkv_lowrank_blocks(up, row0):
    out = jnp.zeros((RWKV_LO_WIN, 2 * D_RWKV), up.dtype)
    for d in range(2):
        out = lax.dynamic_update_slice(out, up[d], (row0 + d * RWKV_RANK, d * D_RWKV))
    return out


def _encoder(x, segs, w):
    (norm1_g, w_in, lru_conv_w, lru_conv_b, lru_gate_w, lru_gate_b, lru_lambda,
     rwkv_mu, rwkv_w0, rwkv_w_up, rwkv_a0, rwkv_a_up, rwkv_g_up, rwkv_k_k, rwkv_k_a,
     rwkv_r_k, rwkv_ln_g, rwkv_ln_b, ssd_conv_w, ssd_conv_b, ssd_dt_bias, ssd_a_log,
     ssd_d, ssd_norm_g, w_out, norm2_g, mlp_w1, mlp_w2, final_norm_g) = w
    depth = w_in.shape[0]
    head_id = jnp.arange(D_RWKV) // HEAD_DIM
    hsel = (head_id[:, None] == jnp.arange(LANE)[None, :]).astype(BF16)
    o1 = LRU_COLS
    o2 = LRU_COLS + RWKV_COLS
    pad_last = lambda a, n: jnp.pad(a, ((0, 0), (0, 0), (0, n - a.shape[2])))
    w_in_bf = w_in.astype(BF16)
    w_in_pad = jnp.concatenate([w_in_bf[:, :, :o1], pad_last(w_in_bf[:, :, o1:o2], RWKV_COLS_PAD),
                                pad_last(w_in_bf[:, :, o2 + D_SSD:], SSD_Z_OFF), w_in_bf[:, :, o2:o2 + D_SSD]],
                               axis=2)
    mlp_w1_bf = mlp_w1.astype(BF16)
    mlp_w2_bf = mlp_w2.astype(BF16)
    for l in range(depth):
        p_lru, p_rwkv, p_ssd = _in_proj(x, norm1_g[l], w_in_pad, l)

        y_lru = _lru_mixer(p_lru, segs, lru_conv_w[l], lru_conv_b[l],
                           _lru_gate_blockdiag(lru_gate_w[l]).astype(BF16), lru_gate_b[l], lru_lambda[l])

        y_rwkv = _rwkv_mixer(
            p_rwkv, segs, _pad_cols(rwkv_mu[l], RWKV_COLS_PAD),
            _rwkv_lowrank_blocks(rwkv_w_up[l], 0).astype(BF16),
            _rwkv_lowrank_blocks(rwkv_a_up[l], RWKV_LO_WIN - 2 * RWKV_RANK).astype(BF16),
            jnp.concatenate([rwkv_w0[l, 0], rwkv_w0[l, 1], rwkv_a0[l, 0], rwkv_a0[l, 1]]).reshape(1, 4 * D_RWKV),
            rwkv_g_up[l].astype(BF16), rwkv_k_k[l], rwkv_k_a[l], rwkv_r_k[l], rwkv_ln_g[l], rwkv_ln_b[l], hsel)

        y_ssd = _ssd_mixer(
            p_ssd, segs, ssd_conv_w[l], ssd_conv_b[l],
            _pad_cols(ssd_dt_bias[l].reshape(1, 2 * SSD_HEADS), LANE),
            _pad_cols(ssd_a_log[l].reshape(1, 2 * SSD_HEADS), LANE),
            jnp.repeat(ssd_d[l], HEAD_DIM).reshape(1, D_SSD), ssd_norm_g[l])

        w_out_l = w_out[l].astype(BF16)
        x = _out_proj(x, y_lru, y_rwkv, y_ssd, w_out_l[:D_LRU], w_out_l[D_LRU:D_LRU + D_RWKV],
                      w_out_l[D_LRU + D_RWKV:])
        x = _mlp(x, norm2_g[l], mlp_w1_bf, mlp_w2_bf, l)
    return _final_norm(x, final_norm_g, segs[0][0] * segs[0][1])


def kernel(x_prompt, x_sample, norm1_g, w_in, lru_conv_w, lru_conv_b, lru_gate_w, lru_gate_b, lru_lambda, rwkv_mu, rwkv_w0, rwkv_w_up, rwkv_a0, rwkv_a_up, rwkv_g_up, rwkv_k_k, rwkv_k_a, rwkv_r_k, rwkv_ln_g, rwkv_ln_b, ssd_conv_w, ssd_conv_b, ssd_dt_bias, ssd_a_log, ssd_d, ssd_norm_g, w_out, norm2_g, mlp_w1, mlp_w2, final_norm_g):
    weights = (norm1_g, w_in, lru_conv_w, lru_conv_b, lru_gate_w, lru_gate_b, lru_lambda,
               rwkv_mu, rwkv_w0, rwkv_w_up, rwkv_a0, rwkv_a_up, rwkv_g_up, rwkv_k_k, rwkv_k_a,
               rwkv_r_k, rwkv_ln_g, rwkv_ln_b, ssd_conv_w, ssd_conv_b, ssd_dt_bias, ssd_a_log,
               ssd_d, ssd_norm_g, w_out, norm2_g, mlp_w1, mlp_w2, final_norm_g)
    bp, lp, d = x_prompt.shape
    bs, ls, _ = x_sample.shape
    segs = ((bp, lp), (bs, ls))
    x = jnp.concatenate([x_prompt.reshape(bp * lp, d), x_sample.reshape(bs * ls, d)], axis=0)
    y_prompt, y_sample = _encoder(x, segs, weights)
    return y_prompt.reshape(bp, lp, d), y_sample.reshape(bs, ls, d)
```

```python
import functools
import math

import jax
import jax.numpy as jnp
from jax import lax
from jax.experimental import pallas as pl
from jax.experimental.pallas import tpu as pltpu

F32 = jnp.float32
BF16 = jnp.bfloat16
MIX_OUT_DTYPE = BF16
PROJ_DTYPE = BF16

D_MODEL = 2048
D_FF = 4 * D_MODEL
NORM_EPS = 1e-6
HEAD_DIM = 64
D_LRU = 512
LRU_C = 8.0
D_RWKV = 768
RWKV_RANK = 96
RWKV_GATE_RANK = 256
RWKV_LO_WIN = 256
RWKV_GN_EPS = 64e-5
RWKV_COLS = 3 * D_RWKV + 4 * RWKV_RANK + RWKV_GATE_RANK
RWKV_COLS_PAD = 3072
D_SSD = 768
SSD_HEADS = 12
SSD_GROUPS = 4
SSD_STATE = 128
SSD_BC = SSD_GROUPS * SSD_STATE
SSD_CONV_DIM = D_SSD + 2 * SSD_BC
SSD_COLS_PAD = 3072
SSD_DT_OFF = SSD_CONV_DIM
SSD_Z_OFF = SSD_COLS_PAD - D_SSD
LRU_COLS = 2 * D_LRU

LANE = 128
SUBLANE = 8
HALO_ROWS = 16
VMEM_LIMIT = 56 * 1024 * 1024

TOK_BLOCK = 256
LRU_BLOCK = 512
SSD_CHUNK = 128
SSD_DECAY_ROWS = 1024
SSD_FAC_COLS = 5 * LANE
RWKV_CHUNK = 64
MM_TM = 1024
IN_PROJ_TN = 512
MLP_TF = 512
FINAL_TM = 512


def _bf(x):
    return x.astype(BF16)


def _mm(a, b):
    return jnp.dot(_bf(a), _bf(b), preferred_element_type=F32)


def _mm_nt(a, b):
    return lax.dot_general(_bf(a), _bf(b), (((1,), (1,)), ((), ())), preferred_element_type=F32)


def _mm_tn(a, b):
    return lax.dot_general(_bf(a), _bf(b), (((0,), (0,)), ((), ())), preferred_element_type=F32)


def _split3(x):
    x1 = _bf(x)
    r1 = x - x1.astype(F32)
    x2 = _bf(r1)
    x3 = _bf(r1 - x2.astype(F32))
    return x1, x2, x3


def _mm_sel_lhs(sel, x):
    x1, x2, x3 = _split3(x)
    d = lambda v: jnp.dot(sel, v, preferred_element_type=F32)
    return d(x1) + d(x2) + d(x3)


def _mm_sel_rhs2(x, sel):
    x1 = _bf(x)
    x2 = _bf(x - x1.astype(F32))
    return jnp.dot(x1, sel, preferred_element_type=F32) + jnp.dot(x2, sel, preferred_element_type=F32)


def _head_sum(x, sel, sel_t):
    return _mm_sel_rhs2(_mm_sel_rhs2(x, sel), sel_t)


def _sigmoid(x):
    return 0.5 * jnp.tanh(0.5 * x) + 0.5


def _softplus(x):
    return jnp.maximum(x, 0.0) + jnp.log1p(jnp.exp(-jnp.abs(x)))


def _silu(x):
    return x * _sigmoid(x)


def _gelu_tanh(x):
    return 0.5 * x * (1.0 + jnp.tanh(math.sqrt(2.0 / math.pi) * (x + 0.044715 * (x * x * x))))


def _rms(x, g):
    return x * lax.rsqrt(jnp.mean(x * x, axis=-1, keepdims=True) + NORM_EPS) * g


def _block_pos(t0, segs):
    pos = None
    slen = None
    base = 0
    for n, length in segs:
        p = lax.rem(t0 - base, length)
        if pos is None:
            pos, slen = p, jnp.int32(length)
        else:
            inside = t0 >= base
            pos = jnp.where(inside, p, pos)
            slen = jnp.where(inside, jnp.int32(length), slen)
        base += n * length
    return pos, slen


def _total_tokens(segs):
    return sum(n * length for n, length in segs)


def _halo_specs(tb, width, t):
    r = tb // HALO_ROWS
    last = t // HALO_ROWS - 1
    prev = pl.BlockSpec((HALO_ROWS, width), lambda i: (jnp.maximum(i * r - 1, 0), 0))
    nxt = pl.BlockSpec((HALO_ROWS, width), lambda i: (jnp.minimum((i + 1) * r, last), 0))
    return prev, nxt


def _halo_rows(pprev_ref, pnext_ref, is_start, is_end, cols=slice(None)):
    prev = pprev_ref[:, cols].astype(F32)
    nxt = pnext_ref[:, cols].astype(F32)
    return (jnp.where(is_start, 0.0, prev[HALO_ROWS - 1:HALO_ROWS]),
            jnp.where(is_end, 0.0, nxt[0:1]), jnp.where(is_end, 0.0, nxt[1:2]))


def _shifted(x, prev_row, next0, next1):
    n = x.shape[0]
    row = lax.broadcasted_iota(jnp.int32, (n, 1), 0)
    xm1 = jnp.where(row == 0, prev_row, pltpu.roll(x, 1, 0))
    xp1 = jnp.where(row == n - 1, next0, pltpu.roll(x, n - 1, 0))
    if next1 is None:
        return xm1, xp1, None
    xp2 = jnp.where(row == n - 2, next0, jnp.where(row == n - 1, next1, pltpu.roll(x, n - 2, 0)))
    return xm1, xp1, xp2


def _conv4(x, prev_row, next0, next1, w, b):
    xm1, xp1, xp2 = _shifted(x, prev_row, next0, next1)
    return w[0:1] * xm1 + w[1:2] * x + w[2:3] * xp1 + w[3:4] * xp2 + b


def _in_proj_kernel(n_lru, n_rwkv, x_ref, g_ref, w_ref, lru_ref, rwkv_ref, ssd_ref, u_ref):
    j = pl.program_id(1)

    @pl.when(j == 0)
    def _():
        u_ref[...] = _bf(_rms(x_ref[...], g_ref[...]))

    def project(o_ref):
        o_ref[...] = jnp.dot(u_ref[...], w_ref[...], preferred_element_type=F32).astype(o_ref.dtype)

    pl.when(j < n_lru)(functools.partial(project, lru_ref))
    pl.when((j >= n_lru) & (j < n_lru + n_rwkv))(functools.partial(project, rwkv_ref))
    pl.when(j >= n_lru + n_rwkv)(functools.partial(project, ssd_ref))


def _in_proj(x, g, w, layer):
    t, d = x.shape
    tm = min(MM_TM, t)
    tn = IN_PROJ_TN
    n_lru, n_rwkv, n_ssd = LRU_COLS // tn, RWKV_COLS_PAD // tn, SSD_COLS_PAD // tn
    return pl.pallas_call(
        functools.partial(_in_proj_kernel, n_lru, n_rwkv),
        grid=(t // tm, n_lru + n_rwkv + n_ssd),
        in_specs=[pl.BlockSpec((tm, d), lambda i, j: (i, 0)),
                  pl.BlockSpec((1, d), lambda i, j: (0, 0)),
                  pl.BlockSpec((None, d, tn), lambda i, j: (layer, 0, j))],
        out_specs=[pl.BlockSpec((tm, tn), lambda i, j: (i, jnp.minimum(j, n_lru - 1))),
                   pl.BlockSpec((tm, tn), lambda i, j: (i, jnp.clip(j - n_lru, 0, n_rwkv - 1))),
                   pl.BlockSpec((tm, tn), lambda i, j: (i, jnp.maximum(j - n_lru - n_rwkv, 0)))],
        out_shape=[jax.ShapeDtypeStruct((t, LRU_COLS), PROJ_DTYPE),
                   jax.ShapeDtypeStruct((t, RWKV_COLS_PAD), PROJ_DTYPE),
                   jax.ShapeDtypeStruct((t, SSD_COLS_PAD), PROJ_DTYPE)],
        scratch_shapes=[pltpu.VMEM((tm, d), BF16)],
        compiler_params=pltpu.CompilerParams(
            dimension_semantics=("arbitrary", "arbitrary"), vmem_limit_bytes=VMEM_LIMIT),
        name="in_proj",
    )(x, g.reshape(1, d), w)


def _out_proj_kernel(x_ref, ya_ref, yb_ref, yc_ref, wa_ref, wb_ref, wc_ref, o_ref):
    acc = jnp.dot(_bf(ya_ref[...]), wa_ref[...], preferred_element_type=F32)
    acc += jnp.dot(_bf(yb_ref[...]), wb_ref[...], preferred_element_type=F32)
    acc += jnp.dot(_bf(yc_ref[...]), wc_ref[...], preferred_element_type=F32)
    o_ref[...] = x_ref[...] + acc


def _out_proj(x, y_lru, y_rwkv, y_ssd, wa, wb, wc):
    t, d = x.shape
    tm = min(MM_TM, t)
    tn = 1024
    return pl.pallas_call(
        _out_proj_kernel,
        grid=(t // tm, d // tn),
        in_specs=[pl.BlockSpec((tm, tn), lambda i, j: (i, j)),
                  pl.BlockSpec((tm, D_LRU), lambda i, j: (i, 0)),
                  pl.BlockSpec((tm, D_RWKV), lambda i, j: (i, 0)),
                  pl.BlockSpec((tm, D_SSD), lambda i, j: (i, 0)),
                  pl.BlockSpec((D_LRU, tn), lambda i, j: (0, j)),
                  pl.BlockSpec((D_RWKV, tn), lambda i, j: (0, j)),
                  pl.BlockSpec((D_SSD, tn), lambda i, j: (0, j))],
        out_specs=pl.BlockSpec((tm, tn), lambda i, j: (i, j)),
        out_shape=jax.ShapeDtypeStruct((t, d), F32),
        compiler_params=pltpu.CompilerParams(
            dimension_semantics=("parallel", "arbitrary"), vmem_limit_bytes=VMEM_LIMIT),
        name="out_proj",
    )(x, y_lru, y_rwkv, y_ssd, wa, wb, wc)


def _mlp_kernel(x_ref, g_ref, w1_ref, w2_ref, o_ref, u_ref):
    @pl.when(pl.program_id(1) == 0)
    def _():
        x = x_ref[...]
        u_ref[...] = _bf(_rms(x, g_ref[...]))
        o_ref[...] = x

    h = jnp.dot(u_ref[...], w1_ref[...], preferred_element_type=F32)
    h = jnp.square(jnp.maximum(h, 0.0))
    o_ref[...] += jnp.dot(_bf(h), w2_ref[...], preferred_element_type=F32)


def _mlp(x, g, w1, w2, layer):
    t, d = x.shape
    ff = w1.shape[2]
    tm = min(MM_TM, t)
    tf = MLP_TF
    return pl.pallas_call(
        _mlp_kernel,
        grid=(t // tm, ff // tf),
        in_specs=[pl.BlockSpec((tm, d), lambda i, f: (i, 0)),
                  pl.BlockSpec((1, d), lambda i, f: (0, 0)),
                  pl.BlockSpec((None, d, tf), lambda i, f: (layer, 0, f)),
                  pl.BlockSpec((None, tf, d), lambda i, f: (layer, f, 0))],
        out_specs=pl.BlockSpec((tm, d), lambda i, f: (i, 0)),
        out_shape=jax.ShapeDtypeStruct((t, d), F32),
        scratch_shapes=[pltpu.VMEM((tm, d), BF16)],
        compiler_params=pltpu.CompilerParams(
            dimension_semantics=("parallel", "arbitrary"), vmem_limit_bytes=VMEM_LIMIT),
        name="mlp",
    )(x, g.reshape(1, d), w1, w2)


def _final_norm_kernel(na, x_ref, g_ref, oa_ref, ob_ref):
    y = _rms(x_ref[...], g_ref[...])
    i = pl.program_id(0)

    @pl.when(i < na)
    def _():
        oa_ref[...] = y

    @pl.when(i >= na)
    def _():
        ob_ref[...] = y


def _final_norm(x, g, ta):
    t, d = x.shape
    tm = min(FINAL_TM, ta, t - ta)
    na = ta // tm
    return pl.pallas_call(
        functools.partial(_final_norm_kernel, na),
        grid=(t // tm,),
        in_specs=[pl.BlockSpec((tm, d), lambda i: (i, 0)),
                  pl.BlockSpec((1, d), lambda i: (0, 0))],
        out_specs=[pl.BlockSpec((tm, d), lambda i: (jnp.minimum(i, na - 1), 0)),
                   pl.BlockSpec((tm, d), lambda i: (jnp.maximum(i - na, 0), 0))],
        out_shape=[jax.ShapeDtypeStruct((ta, d), F32), jax.ShapeDtypeStruct((t - ta, d), F32)],
        compiler_params=pltpu.CompilerParams(dimension_semantics=("arbitrary",)),
        name="final_norm",
    )(x, g.reshape(1, d))


def _scan_rows(a, b, reverse):
    n = a.shape[0]
    row = lax.broadcasted_iota(jnp.int32, (n, 1), 0)
    s = 1
    while s < n:
        shift = n - s if reverse else s
        valid = (row < n - s) if reverse else (row >= s)
        a_sh = pltpu.roll(a, shift, 0)
        b_sh = pltpu.roll(b, shift, 0)
        b = jnp.where(valid, a * b_sh, 0.0) + b
        a = jnp.where(valid, a * a_sh, a)
        s *= 2
    return a, b


def _scan_block(a, b, carry, reverse):
    tiles = range(a.shape[0] // SUBLANE)
    loc = [_scan_rows(a[k * SUBLANE:(k + 1) * SUBLANE], b[k * SUBLANE:(k + 1) * SUBLANE], reverse) for k in tiles]
    hs = [None] * len(tiles)
    for k in (reversed(tiles) if reverse else tiles):
        a_cum, h_loc = loc[k]
        h = h_loc + a_cum * carry
        carry = h[0:1] if reverse else h[SUBLANE - 1:SUBLANE]
        hs[k] = h
    return jnp.concatenate(hs, axis=0), carry


def _lru_fwd_kernel(segs, tb, p_ref, pprev_ref, pnext_ref, cw_ref, cb_ref, wbd_ref, gb_ref, lam_ref,
                    hf_ref, a1_ref, bx1_ref, carry_ref):
    i = pl.program_id(0)
    pos, slen = _block_pos(i * tb, segs)
    is_start = pos == 0
    is_end = pos + tb == slen
    xb = p_ref[:, D_LRU:].astype(F32)
    prev_row, next0, next1 = _halo_rows(pprev_ref, pnext_ref, is_start, is_end, slice(D_LRU, None))
    xc = _conv4(xb, prev_row, next0, next1, cw_ref[...], cb_ref[...])
    gates = _sigmoid(_mm(xc, wbd_ref[...]) + gb_ref[...])
    sp = _softplus(-lam_ref[...])

    def direction(d):
        r = gates[:, (2 * d) * D_LRU:(2 * d + 1) * D_LRU]
        inp = gates[:, (2 * d + 1) * D_LRU:(2 * d + 2) * D_LRU]
        log_a = -LRU_C * r * sp[d:d + 1]
        th = jnp.tanh(log_a)
        one_minus_a2 = -2.0 * th / (1.0 - th)
        return jnp.exp(log_a), jnp.sqrt(one_minus_a2) * (inp * xc)

    a0, bx0 = direction(0)
    a1, bx1 = direction(1)
    a1_ref[...] = a1
    bx1_ref[...] = bx1

    @pl.when(is_start)
    def _():
        carry_ref[...] = jnp.zeros_like(carry_ref)

    h, carry = _scan_block(a0, bx0, carry_ref[...], False)
    hf_ref[...] = h
    carry_ref[...] = carry


def _lru_bwd_kernel(segs, tb, nb, p_ref, a1_ref, bx1_ref, hf_ref, o_ref, carry_ref):
    blk = nb - 1 - pl.program_id(0)
    pos, slen = _block_pos(blk * tb, segs)

    @pl.when(pos + tb == slen)
    def _():
        carry_ref[...] = jnp.zeros_like(carry_ref)

    h, carry = _scan_block(a1_ref[...], bx1_ref[...], carry_ref[...], True)
    carry_ref[...] = carry
    o_ref[...] = ((hf_ref[...] + h) * _gelu_tanh(p_ref[:, :D_LRU].astype(F32))).astype(o_ref.dtype)


def _lru_mixer(p, segs, conv_w, conv_b, w_bd, gate_b, lam):
    t = p.shape[0]
    tb = min(LRU_BLOCK, min(length for _, length in segs))
    nb = t // tb
    prev_spec, next_spec = _halo_specs(tb, LRU_COLS, t)
    const = lambda shape: pl.BlockSpec(shape, lambda i: (0,) * len(shape))
    row_spec = lambda w: pl.BlockSpec((tb, w), lambda i: (i, 0))
    hf, a1, bx1 = pl.pallas_call(
        functools.partial(_lru_fwd_kernel, segs, tb),
        grid=(nb,),
        in_specs=[row_spec(LRU_COLS), prev_spec, next_spec, const((4, D_LRU)), const((1, D_LRU)),
                  const((D_LRU, 4 * D_LRU)), const((1, 4 * D_LRU)), const((2, D_LRU))],
        out_specs=[row_spec(D_LRU)] * 3,
        out_shape=[jax.ShapeDtypeStruct((t, D_LRU), F32)] * 3,
        scratch_shapes=[pltpu.VMEM((1, D_LRU), F32)],
        compiler_params=pltpu.CompilerParams(
            dimension_semantics=("arbitrary",), vmem_limit_bytes=VMEM_LIMIT),
        name="lru_fwd",
    )(p, p, p, conv_w, conv_b.reshape(1, D_LRU), w_bd, gate_b.reshape(1, 4 * D_LRU), lam)
    rev_spec = lambda w: pl.BlockSpec((tb, w), lambda i: (nb - 1 - i, 0))
    return pl.pallas_call(
        functools.partial(_lru_bwd_kernel, segs, tb, nb),
        grid=(nb,),
        in_specs=[rev_spec(LRU_COLS), rev_spec(D_LRU), rev_spec(D_LRU), rev_spec(D_LRU)],
        out_specs=rev_spec(D_LRU),
        out_shape=jax.ShapeDtypeStruct((t, D_LRU), MIX_OUT_DTYPE),
        scratch_shapes=[pltpu.VMEM((1, D_LRU), F32)],
        compiler_params=pltpu.CompilerParams(
            dimension_semantics=("arbitrary",), vmem_limit_bytes=VMEM_LIMIT),
        name="lru_bwd",
    )(p, a1, bx1, hf)


def _ssd_core(reverse, xs, bm, cm, fac, u_t, hexp, ht_ref):
    q = xs.shape[0]
    off = SSD_HEADS if reverse else 0
    row = lax.broadcasted_iota(jnp.int32, (q, q), 0)
    col = lax.broadcasted_iota(jnp.int32, (q, q), 1)
    mask = (col >= row) if reverse else (col <= row)
    dt, u, out_fac, in_fac = (fac[:, n * LANE:(n + 1) * LANE] for n in range(4))
    chunk_decay = fac[0:1, 4 * LANE:5 * LANE]
    lane = lax.broadcasted_iota(jnp.int32, (1, LANE), 1)
    first = lane < HEAD_DIM
    heads = range(SSD_HEADS)
    pairs = range(SSD_HEADS // 2)
    grp = [h // (SSD_HEADS // SSD_GROUPS) for h in heads]
    cols = [slice(off + h, off + h + 1) for h in heads]
    gsl = [slice(g * SSD_STATE, (g + 1) * SSD_STATE) for g in range(SSD_GROUPS)]
    cm_g = [_bf(cm[:, sl]) for sl in gsl]
    bm_g = [_bf(bm[:, sl]) for sl in gsl]
    cb = [_mm_nt(cm_g[g], bm_g[g]) for g in range(SSD_GROUPS)]
    hts = [ht_ref[pr] for pr in pairs]
    hts_bf = [_bf(ht) for ht in hts]
    x_pair = [xs[:, pr * LANE:(pr + 1) * LANE] for pr in pairs]
    seg = [(u_t[c, :] - u[:, c]) if reverse else (u[:, c] - u_t[c, :]) for c in cols]
    m = [_bf(cb[grp[h]] * jnp.exp(jnp.where(mask, seg[h], -1e30))) for h in heads]
    dt_x = _mm_sel_rhs2(dt, hexp)
    dt_in_x = _mm_sel_rhs2(dt * in_fac, hexp)
    out_x = _mm_sel_rhs2(out_fac, hexp)
    psl = [slice(pr * LANE, (pr + 1) * LANE) for pr in pairs]
    xd = [_bf(x_pair[pr] * dt_x[:, psl[pr]]) for pr in pairs]
    xd_in = [_bf(x_pair[pr] * dt_in_x[:, psl[pr]]) for pr in pairs]
    y_diag = [_mm(m[h], xd[h // 2]) for h in heads]
    c_ht = {key: _mm(cm_g[key[1]], hts_bf[key[0]]) for key in sorted({(h // 2, grp[h]) for h in heads})}
    h_new = [chunk_decay[:, cols[h]] * hts[h // 2] + _mm_tn(bm_g[grp[h]], xd_in[h // 2]) for h in heads]
    for pr in pairs:
        ht_ref[pr] = jnp.where(first, h_new[2 * pr], h_new[2 * pr + 1])
    pick = lambda pr, vals: jnp.where(first, vals[0], vals[1])
    return [pick(pr, (y_diag[2 * pr], y_diag[2 * pr + 1]))
            + pick(pr, (c_ht[(pr, grp[2 * pr])], c_ht[(pr, grp[2 * pr + 1])])) * out_x[:, psl[pr]] for pr in pairs]


def _ssd_decay_kernel(q, dtraw_ref, dtb_ref, alog_ref, fac_ref, ut_ref):
    ks = range(dtraw_ref.shape[0] // q)
    rows = [slice(k * q, (k + 1) * q) for k in ks]
    lane = lax.broadcasted_iota(jnp.int32, (1, LANE), 1)
    fwd_lane = lane < SSD_HEADS
    a_head = jnp.where(lane < 2 * SSD_HEADS, -jnp.exp(alog_ref[...]), 0.0)
    row = lax.broadcasted_iota(jnp.int32, (q, q), 0)
    col = lax.broadcasted_iota(jnp.int32, (q, q), 1)
    lower = _bf((col <= row).astype(F32))
    dt = [_softplus(dtraw_ref[r, :].astype(F32) + dtb_ref[...]) for r in rows]
    da = [dt[k] * a_head for k in ks]
    cum = [_mm_sel_lhs(lower, da[k]) for k in ks]
    total = [cum[k][q - 1:q] for k in ks]
    u = [jnp.where(fwd_lane, cum[k], cum[k] - da[k]) for k in ks]
    e_u = [jnp.exp(u[k]) for k in ks]
    e_rest = [jnp.exp(total[k] - u[k]) for k in ks]
    for k in ks:
        pieces = (dt[k], u[k], jnp.where(fwd_lane, e_u[k], e_rest[k]), jnp.where(fwd_lane, e_rest[k], e_u[k]),
                  jnp.broadcast_to(jnp.exp(total[k]), (q, LANE)))
        for n, piece in enumerate(pieces):
            fac_ref[rows[k], n * LANE:(n + 1) * LANE] = piece
        ut_ref[:, rows[k]] = u[k].T


def _ssd_fwd_kernel(segs, q, p_ref, pprev_ref, pnext_ref, cw_ref, cb_ref, fac_ref, ut_ref, hexp_ref, dskip_ref,
                    xconv_ref, yacc_ref, ht_ref):
    i = pl.program_id(0)
    pos, slen = _block_pos(i * q, segs)
    is_start = pos == 0
    is_end = pos + q == slen
    prev_row, next0, next1 = _halo_rows(pprev_ref, pnext_ref, is_start, is_end)
    xbc = _silu(_conv4(p_ref[...].astype(F32), prev_row, next0, next1, cw_ref[...], cb_ref[...]))
    xconv_ref[...] = xbc
    xs = xbc[:, :D_SSD]

    @pl.when(is_start)
    def _():
        ht_ref[...] = jnp.zeros_like(ht_ref)

    ys = _ssd_core(False, xs, xbc[:, D_SSD:D_SSD + SSD_BC], xbc[:, D_SSD + SSD_BC:], fac_ref[...], ut_ref[...],
                   hexp_ref[...], ht_ref)
    for pr, y in enumerate(ys):
        sl = slice(pr * LANE, (pr + 1) * LANE)
        yacc_ref[:, sl] = y + dskip_ref[:, sl] * xs[:, sl]


def _ssd_bwd_kernel(segs, q, nb, z_ref, xconv_ref, yacc_ref, fac_ref, ut_ref, hexp_ref, ng_ref, o_ref, ht_ref):
    blk = nb - 1 - pl.program_id(0)
    pos, slen = _block_pos(blk * q, segs)

    @pl.when(pos + q == slen)
    def _():
        ht_ref[...] = jnp.zeros_like(ht_ref)

    xbc = xconv_ref[...]
    ys = _ssd_core(True, xbc[:, :D_SSD], xbc[:, D_SSD:D_SSD + SSD_BC], xbc[:, D_SSD + SSD_BC:], fac_ref[...],
                   ut_ref[...], hexp_ref[...], ht_ref)
    y = jnp.concatenate(ys, axis=1) + yacc_ref[...]
    o_ref[...] = _rms(y * _silu(z_ref[...].astype(F32)), ng_ref[...]).astype(o_ref.dtype)


def _ssd_mixer(p, segs, conv_w, conv_b, dt_bias, a_log, d_skip, norm_g):
    t = p.shape[0]
    q = SSD_CHUNK
    nb = t // q
    prev_spec, next_spec = _halo_specs(q, SSD_CONV_DIM, t)
    const = lambda shape: pl.BlockSpec(shape, lambda i: (0,) * len(shape))
    row_spec = lambda w: pl.BlockSpec((q, w), lambda i: (i, 0))
    state = pltpu.VMEM((SSD_HEADS // 2, SSD_STATE, LANE), F32)
    rows = min(SSD_DECAY_ROWS, t)
    dt_col = SSD_DT_OFF // LANE
    head_expand = lambda off: (jnp.arange(LANE)[:, None] == off + jnp.arange(D_SSD)[None, :] // HEAD_DIM).astype(BF16)
    fac, u_t = pl.pallas_call(
        functools.partial(_ssd_decay_kernel, q),
        grid=(t // rows,),
        in_specs=[pl.BlockSpec((rows, LANE), lambda i: (i, dt_col)), const((1, LANE)), const((1, LANE))],
        out_specs=[pl.BlockSpec((rows, SSD_FAC_COLS), lambda i: (i, 0)), pl.BlockSpec((LANE, rows), lambda i: (0, i))],
        out_shape=[jax.ShapeDtypeStruct((t, SSD_FAC_COLS), F32), jax.ShapeDtypeStruct((LANE, t), F32)],
        compiler_params=pltpu.CompilerParams(
            dimension_semantics=("parallel",), vmem_limit_bytes=VMEM_LIMIT),
        name="ssd_decay",
    )(p, dt_bias, a_log)
    xconv, yacc = pl.pallas_call(
        functools.partial(_ssd_fwd_kernel, segs, q),
        grid=(nb,),
        in_specs=[row_spec(SSD_CONV_DIM), prev_spec, next_spec, const((4, SSD_CONV_DIM)),
                  const((1, SSD_CONV_DIM)), row_spec(SSD_FAC_COLS), pl.BlockSpec((LANE, q), lambda i: (0, i)),
                  const((LANE, D_SSD)), const((1, D_SSD))],
        out_specs=[row_spec(SSD_CONV_DIM), row_spec(D_SSD)],
        out_shape=[jax.ShapeDtypeStruct((t, SSD_CONV_DIM), F32), jax.ShapeDtypeStruct((t, D_SSD), F32)],
        scratch_shapes=[state],
        compiler_params=pltpu.CompilerParams(
            dimension_semantics=("arbitrary",), vmem_limit_bytes=VMEM_LIMIT),
        name="ssd_fwd",
    )(p, p, p, conv_w, conv_b.reshape(1, SSD_CONV_DIM), fac, u_t, head_expand(0), d_skip)
    rev_spec = lambda w: pl.BlockSpec((q, w), lambda i: (nb - 1 - i, 0))
    return pl.pallas_call(
        functools.partial(_ssd_bwd_kernel, segs, q, nb),
        grid=(nb,),
        in_specs=[pl.BlockSpec((q, D_SSD), lambda i: (nb - 1 - i, SSD_Z_OFF // D_SSD)),
                  rev_spec(SSD_CONV_DIM), rev_spec(D_SSD), rev_spec(SSD_FAC_COLS),
                  pl.BlockSpec((LANE, q), lambda i: (0, nb - 1 - i)), const((LANE, D_SSD)), const((1, D_SSD))],
        out_specs=rev_spec(D_SSD),
        out_shape=jax.ShapeDtypeStruct((t, D_SSD), MIX_OUT_DTYPE),
        scratch_shapes=[state],
        compiler_params=pltpu.CompilerParams(
            dimension_semantics=("arbitrary",), vmem_limit_bytes=VMEM_LIMIT),
        name="ssd_bwd",
    )(p, xconv, yacc, fac, u_t, head_expand(SSD_HEADS), norm_g.reshape(1, D_SSD))


def _rwkv_pre_kernel(segs, tb, p_ref, pprev_ref, pnext_ref, mu_ref, ww_ref, wa_ref, w0a0_ref, gup_ref, kk_ref_w,
                     ka_ref, rk_ref, hsel_ref, hselt_ref,
                     r_out, v_out, kk_out, lw0_out, lw1_out, kd0_out, kd1_out, b0_out, b1_out, g_out, bon_out):
    i = pl.program_id(0)
    pos, slen = _block_pos(i * tb, segs)
    p = p_ref[...].astype(F32)
    prev_row, next_row, _ = _halo_rows(pprev_ref, pnext_ref, pos == 0, pos + tb == slen)
    pm1, pp1, _ = _shifted(p, prev_row, next_row, None)
    mu = mu_ref[...]
    ps = p + mu[0:1] * (pm1 - p) + mu[1:2] * (pp1 - p)
    c = D_RWKV
    r = ps[:, 0:c]
    k = ps[:, c:2 * c]
    v = ps[:, 2 * c:3 * c]
    lo0 = 3 * c
    w_win = ps[:, lo0:lo0 + RWKV_LO_WIN]
    a_win = ps[:, lo0 + 4 * RWKV_RANK - RWKV_LO_WIN:lo0 + 4 * RWKV_RANK]
    g_lo = ps[:, lo0 + 4 * RWKV_RANK:lo0 + 4 * RWKV_RANK + RWKV_GATE_RANK]
    pre_w = _mm(jnp.tanh(w_win), ww_ref[...]) + w0a0_ref[:, :2 * c]
    pre_a = _mm(a_win, wa_ref[...]) + w0a0_ref[:, 2 * c:]
    g = _mm(_sigmoid(g_lo), gup_ref[...])
    hsel = hsel_ref[...]
    hselt = hselt_ref[...]
    kk = k * kk_ref_w[...]
    ss = _head_sum(kk * kk, hsel, hselt)
    kk = kk / jnp.maximum(jnp.sqrt(ss), 1e-12)
    kd_sum = None
    outs = ((lw0_out, kd0_out, b0_out), (lw1_out, kd1_out, b1_out))
    for d in range(2):
        w_pre = pre_w[:, d * c:(d + 1) * c]
        a = _sigmoid(pre_a[:, d * c:(d + 1) * c])
        outs[d][0][...] = -math.exp(-0.5) * _sigmoid(w_pre)
        kd = k * (1.0 + (a - 1.0) * ka_ref[...])
        outs[d][1][...] = kd.astype(outs[d][1].dtype)
        outs[d][2][...] = (kk * a).astype(outs[d][2].dtype)
        kd_sum = kd if kd_sum is None else kd_sum + kd
    r_out[...] = r.astype(r_out.dtype)
    v_out[...] = v.astype(v_out.dtype)
    kk_out[...] = kk.astype(kk_out.dtype)
    g_out[...] = g
    bon_out[...] = _head_sum(r * kd_sum * rk_ref[...], hsel, hselt) * v


def _rwkv_chunk_problems(reverse, cl, r_ref, v_ref, kk_ref, lw_ref, kd_ref, b_ref):
    n2 = 2 * cl
    rowc = lax.broadcasted_iota(jnp.int32, (cl, cl), 0)
    colc = lax.broadcasted_iota(jnp.int32, (cl, cl), 1)
    tri = _bf(((colc >= rowc) if reverse else (colc <= rowc)).astype(F32))
    lw = lw_ref[...]
    cum = _mm_sel_lhs(tri, lw)
    total = cum[0:1] if reverse else cum[cl - 1:cl]
    g_in = jnp.exp(cum)
    g_prev = jnp.exp(cum - lw)
    g_inv = jnp.exp(-cum)
    g_end = jnp.exp(total - cum)
    g_tot = jnp.exp(total)
    kk = kk_ref[...]
    zt = -kk * g_prev
    rt = r_ref[...] * g_in
    bh = b_ref[...] * g_inv
    kh = kd_ref[...] * g_inv
    bg = b_ref[...] * g_end
    kg = kd_ref[...] * g_end
    v = v_ref[...]

    row = lax.broadcasted_iota(jnp.int32, (n2, n2), 0)
    col = lax.broadcasted_iota(jnp.int32, (n2, n2), 1)
    same = (row >= cl) == (col >= cl)
    rt_i = jnp.where(row >= cl, row - cl, row)
    ct_i = jnp.where(col >= cl, col - cl, col)
    if reverse:
        m_strict = same & (ct_i > rt_i)
        m_incl = same & (ct_i >= rt_i)
    else:
        m_strict = same & (ct_i < rt_i)
        m_incl = same & (ct_i <= rt_i)
    eye = row == col
    first = lax.broadcasted_iota(jnp.int32, (1, LANE), 1) < HEAD_DIM
    pair_mask = same & ((rt_i >> 1) == (ct_i >> 1))
    join_masks = []
    shift = 1
    while (1 << shift) < cl:
        later, earlier = (ct_i, rt_i) if reverse else (rt_i, ct_i)
        join_masks.append(same & ((rt_i >> (shift + 1)) == (ct_i >> (shift + 1)))
                          & (((later >> shift) & 1) == 1) & (((earlier >> shift) & 1) == 0))
        shift += 1

    def stack(x):
        return jnp.concatenate([jnp.where(first, x, 0.0), jnp.where(first, 0.0, x)], axis=0)

    masks = dict(strict=m_strict, incl=m_incl, eye=eye, pair=pair_mask, joins=join_masks)
    problems = []
    for pr in range(D_RWKV // LANE):
        sl = slice(pr * LANE, (pr + 1) * LANE)
        ztm, rtm, bhm, khm, bgm, kgm, vm = (_bf(stack(x[:, sl])) for x in (zt, rt, bh, kh, bg, kg, v))
        problems.append(dict(masks, ztm=ztm, rtm=rtm, bhm=bhm, khm=khm, bgm=bgm, kgm=kgm, vm=vm,
                             g_tot=g_tot[:, sl]))
    return problems


def _rwkv_solve(problems, sts):
    ps = range(len(problems))
    n2 = problems[0]['ztm'].shape[0]
    q = problems
    zr = [jnp.concatenate([q[i]['ztm'], q[i]['rtm']], axis=0) for i in ps]
    bk = [jnp.concatenate([q[i]['bhm'], q[i]['khm']], axis=0) for i in ps]
    g = [_mm_nt(zr[i], bk[i]) for i in ps]
    a_ab = [jnp.where(q[i]['strict'], g[i][:n2, :n2], 0.0) for i in ps]
    a_ak = [_bf(jnp.where(q[i]['strict'], g[i][:n2, n2:], 0.0)) for i in ps]
    a_rb = [_bf(jnp.where(q[i]['incl'], g[i][n2:, :n2], 0.0)) for i in ps]
    a_rk = [_bf(jnp.where(q[i]['incl'], g[i][n2:, n2:], 0.0)) for i in ps]
    sa_v = [_mm(a_ak[i], q[i]['vm']) for i in ps]
    tinv = [jnp.where(q[i]['eye'], 1.0, jnp.where(q[i]['pair'], a_ab[i], 0.0)) for i in ps]
    for level in range(len(q[0]['joins'])):
        tb16 = [_bf(tinv[i]) for i in ps]
        half = [_mm(tb16[i], jnp.where(q[i]['joins'][level], a_ab[i], 0.0)) for i in ps]
        tinv = [tinv[i] + _mm(half[i], tb16[i]) for i in ps]
    x = [_bf(_mm(tinv[i], jnp.concatenate([q[i]['ztm'], _bf(sa_v[i])], axis=1))) for i in ps]
    z_eff = [x[i][:, :LANE] for i in ps]
    bg_t = [q[i]['bgm'].T for i in ps]
    kg_t = [q[i]['kgm'].T for i in ps]
    r_eff = [_bf(q[i]['rtm'].astype(F32) + _mm(a_rb[i], z_eff[i])) for i in ps]
    p_mat = [_bf(jnp.where(q[i]['eye'], q[i]['g_tot'], 0.0) + _mm(bg_t[i], z_eff[i])) for i in ps]
    rhs = [jnp.concatenate([x[i][:, LANE:], q[i]['vm'], _bf(sts[i])], axis=0) for i in ps]
    lhs = [jnp.concatenate([jnp.concatenate([a_rb[i], a_rk[i], r_eff[i]], axis=1),
                            jnp.concatenate([bg_t[i], kg_t[i], p_mat[i]], axis=1)], axis=0) for i in ps]
    out = [_mm(lhs[i], rhs[i]) for i in ps]
    return [out[i][:n2] for i in ps], [out[i][n2:] for i in ps]


def _rwkv_scan_kernel(segs, cl, nb, rf_ref, vf_ref, kkf_ref, lwf_ref, kdf_ref, bf_ref,
                      rb_ref, vb_ref, kkb_ref, lwb_ref, kdb_ref, bb_ref, yf_ref, yb_ref, stf_ref, stb_ref):
    i = pl.program_id(0)
    pos_f, _ = _block_pos(i * cl, segs)
    pos_b, slen_b = _block_pos((nb - 1 - i) * cl, segs)

    @pl.when(pos_f == 0)
    def _():
        stf_ref[...] = jnp.zeros_like(stf_ref)

    @pl.when(pos_b + cl == slen_b)
    def _():
        stb_ref[...] = jnp.zeros_like(stb_ref)

    npair = D_RWKV // LANE
    problems = (_rwkv_chunk_problems(False, cl, rf_ref, vf_ref, kkf_ref, lwf_ref, kdf_ref, bf_ref)
                + _rwkv_chunk_problems(True, cl, rb_ref, vb_ref, kkb_ref, lwb_ref, kdb_ref, bb_ref))
    sts = [stf_ref[pr] for pr in range(npair)] + [stb_ref[pr] for pr in range(npair)]
    ym, st_new = _rwkv_solve(problems, sts)
    for pr in range(npair):
        sl = slice(pr * LANE, (pr + 1) * LANE)
        stf_ref[pr] = st_new[pr]
        stb_ref[pr] = st_new[npair + pr]
        yf_ref[:, sl] = ym[pr][:cl] + ym[pr][cl:]
        yb_ref[:, sl] = ym[npair + pr][:cl] + ym[npair + pr][cl:]


def _rwkv_post_kernel(yf_ref, yb_ref, bon_ref, g_ref, lng_ref, lnb_ref, hsel_ref, hselt_ref, o_ref):
    y = yf_ref[...] + yb_ref[...]
    hsel = hsel_ref[...]
    hselt = hselt_ref[...]
    mean = _head_sum(y, hsel, hselt) * (1.0 / HEAD_DIM)
    d = y - mean
    var = _head_sum(d * d, hsel, hselt) * (1.0 / HEAD_DIM)
    y = d * lax.rsqrt(var + RWKV_GN_EPS) * lng_ref[...] + lnb_ref[...]
    o_ref[...] = ((y + bon_ref[...]) * g_ref[...]).astype(o_ref.dtype)


def _rwkv_mixer(p, segs, mu, w_w, w_a, w0a0, g_up, k_k, k_a, r_k, ln_g, ln_b, hsel):
    hselt = hsel.T
    t = p.shape[0]
    c = D_RWKV
    tb = min(TOK_BLOCK, min(length for _, length in segs))
    prev_spec, next_spec = _halo_specs(tb, RWKV_COLS_PAD, t)
    const = lambda shape: pl.BlockSpec(shape, lambda i: (0,) * len(shape))
    row_spec = lambda n, w: pl.BlockSpec((n, w), lambda i: (i, 0))
    vec = lambda a: a.reshape(1, c)
    tok = jax.ShapeDtypeStruct((t, c), F32)
    opd = jax.ShapeDtypeStruct((t, c), BF16)
    r, v, kk, lw0, lw1, kd0, kd1, b0, b1, g, bon = pl.pallas_call(
        functools.partial(_rwkv_pre_kernel, segs, tb),
        grid=(t // tb,),
        in_specs=[row_spec(tb, RWKV_COLS_PAD), prev_spec, next_spec, const((2, RWKV_COLS_PAD)),
                  const((RWKV_LO_WIN, 2 * c)), const((RWKV_LO_WIN, 2 * c)), const((1, 4 * c)),
                  const((RWKV_GATE_RANK, c)), const((1, c)), const((1, c)), const((1, c)),
                  const((c, LANE)), const((LANE, c))],
        out_specs=[row_spec(tb, c)] * 11,
        out_shape=[opd, opd, opd, tok, tok, opd, opd, opd, opd, tok, tok],
        compiler_params=pltpu.CompilerParams(
            dimension_semantics=("parallel",), vmem_limit_bytes=VMEM_LIMIT),
        name="rwkv_pre",
    )(p, p, p, mu, w_w, w_a, w0a0, g_up, vec(k_k), vec(k_a), vec(r_k), hsel, hselt)

    cl = RWKV_CHUNK
    nb = t // cl
    fwd_spec = pl.BlockSpec((cl, c), lambda i: (i, 0))
    bwd_spec = pl.BlockSpec((cl, c), lambda i: (nb - 1 - i, 0))
    state = pltpu.VMEM((c // LANE, LANE, LANE), F32)
    ys = pl.pallas_call(
        functools.partial(_rwkv_scan_kernel, segs, cl, nb),
        grid=(nb,),
        in_specs=[fwd_spec] * 6 + [bwd_spec] * 6,
        out_specs=[fwd_spec, bwd_spec],
        out_shape=[tok, tok],
        scratch_shapes=[state, state],
        compiler_params=pltpu.CompilerParams(
            dimension_semantics=("arbitrary",), vmem_limit_bytes=VMEM_LIMIT),
        name="rwkv_scan",
    )(r, v, kk, lw0, kd0, b0, r, v, kk, lw1, kd1, b1)

    return pl.pallas_call(
        _rwkv_post_kernel,
        grid=(t // tb,),
        in_specs=[row_spec(tb, c)] * 4 + [const((1, c)), const((1, c)), const((c, LANE)), const((LANE, c))],
        out_specs=row_spec(tb, c),
        out_shape=jax.ShapeDtypeStruct((t, c), MIX_OUT_DTYPE),
        compiler_params=pltpu.CompilerParams(
            dimension_semantics=("parallel",), vmem_limit_bytes=VMEM_LIMIT),
        name="rwkv_post",
    )(ys[0], ys[1], bon, g, vec(ln_g), vec(ln_b), hsel, hselt)


def _pad_cols(w, n):
    return jnp.pad(w, ((0, 0), (0, n - w.shape[1])))


def _lru_gate_blockdiag(gate_w):
    nh = gate_w.shape[2]
    eye = jnp.eye(nh, dtype=gate_w.dtype)
    w = jnp.einsum('dghij,hk->hidgkj', gate_w, eye)
    return w.reshape(D_LRU, 4 * D_LRU)


def _rwkv_lowrank_blocks(up, row0):
    out = jnp.zeros((RWKV_LO_WIN, 2 * D_RWKV), up.dtype)
    for d in range(2):
        out = lax.dynamic_update_slice(out, up[d], (row0 + d * RWKV_RANK, d * D_RWKV))
    return out


def _encoder(x, segs, w):
    (norm1_g, w_in, lru_conv_w, lru_conv_b, lru_gate_w, lru_gate_b, lru_lambda,
     rwkv_mu, rwkv_w0, rwkv_w_up, rwkv_a0, rwkv_a_up, rwkv_g_up, rwkv_k_k, rwkv_k_a,
     rwkv_r_k, rwkv_ln_g, rwkv_ln_b, ssd_conv_w, ssd_conv_b, ssd_dt_bias, ssd_a_log,
     ssd_d, ssd_norm_g, w_out, norm2_g, mlp_w1, mlp_w2, final_norm_g) = w
    depth = w_in.shape[0]
    head_id = jnp.arange(D_RWKV) // HEAD_DIM
    hsel = (head_id[:, None] == jnp.arange(LANE)[None, :]).astype(BF16)
    o1 = LRU_COLS
    o2 = LRU_COLS + RWKV_COLS
    pad_last = lambda a, n: jnp.pad(a, ((0, 0), (0, 0), (0, n - a.shape[2])))
    w_in_bf = w_in.astype(BF16)
    w_in_pad = jnp.concatenate([w_in_bf[:, :, :o1], pad_last(w_in_bf[:, :, o1:o2], RWKV_COLS_PAD),
                                pad_last(w_in_bf[:, :, o2 + D_SSD:], SSD_Z_OFF), w_in_bf[:, :, o2:o2 + D_SSD]],
                               axis=2)
    mlp_w1_bf = mlp_w1.astype(BF16)
    mlp_w2_bf = mlp_w2.astype(BF16)
    for l in range(depth):
        p_lru, p_rwkv, p_ssd = _in_proj(x, norm1_g[l], w_in_pad, l)

        y_lru = _lru_mixer(p_lru, segs, lru_conv_w[l], lru_conv_b[l],
                           _lru_gate_blockdiag(lru_gate_w[l]).astype(BF16), lru_gate_b[l], lru_lambda[l])

        y_rwkv = _rwkv_mixer(
            p_rwkv, segs, _pad_cols(rwkv_mu[l], RWKV_COLS_PAD),
            _rwkv_lowrank_blocks(rwkv_w_up[l], 0).astype(BF16),
            _rwkv_lowrank_blocks(rwkv_a_up[l], RWKV_LO_WIN - 2 * RWKV_RANK).astype(BF16),
            jnp.concatenate([rwkv_w0[l, 0], rwkv_w0[l, 1], rwkv_a0[l, 0], rwkv_a0[l, 1]]).reshape(1, 4 * D_RWKV),
            rwkv_g_up[l].astype(BF16), rwkv_k_k[l], rwkv_k_a[l], rwkv_r_k[l], rwkv_ln_g[l], rwkv_ln_b[l], hsel)

        y_ssd = _ssd_mixer(
            p_ssd, segs, ssd_conv_w[l], ssd_conv_b[l],
            _pad_cols(ssd_dt_bias[l].reshape(1, 2 * SSD_HEADS), LANE),
            _pad_cols(ssd_a_log[l].reshape(1, 2 * SSD_HEADS), LANE),
            jnp.repeat(ssd_d[l], HEAD_DIM).reshape(1, D_SSD), ssd_norm_g[l])

        w_out_l = w_out[l].astype(BF16)
        x = _out_proj(x, y_lru, y_rwkv, y_ssd, w_out_l[:D_LRU], w_out_l[D_LRU:D_LRU + D_RWKV],
                      w_out_l[D_LRU + D_RWKV:])
        x = _mlp(x, norm2_g[l], mlp_w1_bf, mlp_w2_bf, l)
    return _final_norm(x, final_norm_g, segs[0][0] * segs[0][1])


def kernel(x_prompt, x_sample, norm1_g, w_in, lru_conv_w, lru_conv_b, lru_gate_w, lru_gate_b, lru_lambda, rwkv_mu, rwkv_w0, rwkv_w_up, rwkv_a0, rwkv_a_up, rwkv_g_up, rwkv_k_k, rwkv_k_a, rwkv_r_k, rwkv_ln_g, rwkv_ln_b, ssd_conv_w, ssd_conv_b, ssd_dt_bias, ssd_a_log, ssd_d, ssd_norm_g, w_out, norm2_g, mlp_w1, mlp_w2, final_norm_g):
    weights = (norm1_g, w_in, lru_conv_w, lru_conv_b, lru_gate_w, lru_gate_b, lru_lambda,
               rwkv_mu, rwkv_w0, rwkv_w_up, rwkv_a0, rwkv_a_up, rwkv_g_up, rwkv_k_k, rwkv_k_a,
               rwkv_r_k, rwkv_ln_g, rwkv_ln_b, ssd_conv_w, ssd_conv_b, ssd_dt_bias, ssd_a_log,
               ssd_d, ssd_norm_g, w_out, norm2_g, mlp_w1, mlp_w2, final_norm_g)
    bp, lp, d = x_prompt.shape
    bs, ls, _ = x_sample.shape
    segs = ((bp, lp), (bs, ls))
    x = jnp.concatenate([x_prompt.reshape(bp * lp, d), x_sample.reshape(bs * ls, d)], axis=0)
    y_prompt, y_sample = _encoder(x, segs, weights)
    return y_prompt.reshape(bp, lp, d), y_sample.reshape(bs, ls, d)
```

```python
import functools
import math

import jax
import jax.numpy as jnp
from jax import lax
from jax.experimental import pallas as pl
from jax.experimental.pallas import tpu as pltpu

F32 = jnp.float32
BF16 = jnp.bfloat16
MIX_OUT_DTYPE = BF16
PROJ_DTYPE = F32

D_MODEL = 2048
D_FF = 4 * D_MODEL
NORM_EPS = 1e-6
HEAD_DIM = 64
D_LRU = 512
LRU_C = 8.0
D_RWKV = 768
RWKV_RANK = 96
RWKV_GATE_RANK = 256
RWKV_LO_WIN = 256
RWKV_GN_EPS = 64e-5
RWKV_COLS = 3 * D_RWKV + 4 * RWKV_RANK + RWKV_GATE_RANK
RWKV_COLS_PAD = 3072
D_SSD = 768
SSD_HEADS = 12
SSD_GROUPS = 4
SSD_STATE = 128
SSD_BC = SSD_GROUPS * SSD_STATE
SSD_CONV_DIM = D_SSD + 2 * SSD_BC
SSD_COLS_PAD = 3072
SSD_DT_OFF = SSD_CONV_DIM
SSD_Z_OFF = SSD_COLS_PAD - D_SSD
LRU_COLS = 2 * D_LRU

LANE = 128
SUBLANE = 8
HALO_ROWS = 16
VMEM_LIMIT = 56 * 1024 * 1024

TOK_BLOCK = 256
LRU_BLOCK = 512
SSD_CHUNK = 128
SSD_DECAY_ROWS = 1024
SSD_FAC_COLS = 5 * LANE
RWKV_CHUNK = 64
MM_TM = 1024
IN_PROJ_TN = 512
MLP_TF = 512
FINAL_TM = 512


def _bf(x):
    return x.astype(BF16)


def _mm(a, b):
    return jnp.dot(_bf(a), _bf(b), preferred_element_type=F32)


def _mm_nt(a, b):
    return lax.dot_general(_bf(a), _bf(b), (((1,), (1,)), ((), ())), preferred_element_type=F32)


def _mm_tn(a, b):
    return lax.dot_general(_bf(a), _bf(b), (((0,), (0,)), ((), ())), preferred_element_type=F32)


def _split3(x):
    x1 = _bf(x)
    r1 = x - x1.astype(F32)
    x2 = _bf(r1)
    x3 = _bf(r1 - x2.astype(F32))
    return x1, x2, x3


def _mm_sel_lhs(sel, x):
    x1, x2, x3 = _split3(x)
    d = lambda v: jnp.dot(sel, v, preferred_element_type=F32)
    return d(x1) + d(x2) + d(x3)


def _mm_sel_rhs2(x, sel):
    x1 = _bf(x)
    x2 = _bf(x - x1.astype(F32))
    return jnp.dot(x1, sel, preferred_element_type=F32) + jnp.dot(x2, sel, preferred_element_type=F32)


def _head_sum(x, sel, sel_t):
    return _mm_sel_rhs2(_mm_sel_rhs2(x, sel), sel_t)


def _sigmoid(x):
    return 0.5 * jnp.tanh(0.5 * x) + 0.5


def _softplus(x):
    return jnp.maximum(x, 0.0) + jnp.log1p(jnp.exp(-jnp.abs(x)))


def _silu(x):
    return x * _sigmoid(x)


def _gelu_tanh(x):
    return 0.5 * x * (1.0 + jnp.tanh(math.sqrt(2.0 / math.pi) * (x + 0.044715 * (x * x * x))))


def _rms(x, g):
    return x * lax.rsqrt(jnp.mean(x * x, axis=-1, keepdims=True) + NORM_EPS) * g


def _block_pos(t0, segs):
    pos = None
    slen = None
    base = 0
    for n, length in segs:
        p = lax.rem(t0 - base, length)
        if pos is None:
            pos, slen = p, jnp.int32(length)
        else:
            inside = t0 >= base
            pos = jnp.where(inside, p, pos)
            slen = jnp.where(inside, jnp.int32(length), slen)
        base += n * length
    return pos, slen


def _total_tokens(segs):
    return sum(n * length for n, length in segs)


def _halo_specs(tb, width, t):
    r = tb // HALO_ROWS
    last = t // HALO_ROWS - 1
    prev = pl.BlockSpec((HALO_ROWS, width), lambda i: (jnp.maximum(i * r - 1, 0), 0))
    nxt = pl.BlockSpec((HALO_ROWS, width), lambda i: (jnp.minimum((i + 1) * r, last), 0))
    return prev, nxt


def _halo_rows(pprev_ref, pnext_ref, is_start, is_end, cols=slice(None)):
    prev = pprev_ref[:, cols].astype(F32)
    nxt = pnext_ref[:, cols].astype(F32)
    return (jnp.where(is_start, 0.0, prev[HALO_ROWS - 1:HALO_ROWS]),
            jnp.where(is_end, 0.0, nxt[0:1]), jnp.where(is_end, 0.0, nxt[1:2]))


def _shifted(x, prev_row, next0, next1):
    n = x.shape[0]
    row = lax.broadcasted_iota(jnp.int32, (n, 1), 0)
    xm1 = jnp.where(row == 0, prev_row, pltpu.roll(x, 1, 0))
    xp1 = jnp.where(row == n - 1, next0, pltpu.roll(x, n - 1, 0))
    if next1 is None:
        return xm1, xp1, None
    xp2 = jnp.where(row == n - 2, next0, jnp.where(row == n - 1, next1, pltpu.roll(x, n - 2, 0)))
    return xm1, xp1, xp2


def _conv4(x, prev_row, next0, next1, w, b):
    xm1, xp1, xp2 = _shifted(x, prev_row, next0, next1)
    return w[0:1] * xm1 + w[1:2] * x + w[2:3] * xp1 + w[3:4] * xp2 + b


def _in_proj_kernel(n_lru, n_rwkv, x_ref, g_ref, w_ref, lru_ref, rwkv_ref, ssd_ref, u_ref):
    j = pl.program_id(1)

    @pl.when(j == 0)
    def _():
        u_ref[...] = _bf(_rms(x_ref[...], g_ref[...]))

    def project(o_ref):
        o_ref[...] = jnp.dot(u_ref[...], w_ref[...], preferred_element_type=F32).astype(o_ref.dtype)

    pl.when(j < n_lru)(functools.partial(project, lru_ref))
    pl.when((j >= n_lru) & (j < n_lru + n_rwkv))(functools.partial(project, rwkv_ref))
    pl.when(j >= n_lru + n_rwkv)(functools.partial(project, ssd_ref))


def _in_proj(x, g, w, layer):
    t, d = x.shape
    tm = min(MM_TM, t)
    tn = IN_PROJ_TN
    n_lru, n_rwkv, n_ssd = LRU_COLS // tn, RWKV_COLS_PAD // tn, SSD_COLS_PAD // tn
    return pl.pallas_call(
        functools.partial(_in_proj_kernel, n_lru, n_rwkv),
        grid=(t // tm, n_lru + n_rwkv + n_ssd),
        in_specs=[pl.BlockSpec((tm, d), lambda i, j: (i, 0)),
                  pl.BlockSpec((1, d), lambda i, j: (0, 0)),
                  pl.BlockSpec((None, None, d, tn), lambda i, j: (layer, j, 0, 0))],
        out_specs=[pl.BlockSpec((tm, tn), lambda i, j: (i, jnp.minimum(j, n_lru - 1))),
                   pl.BlockSpec((tm, tn), lambda i, j: (i, jnp.clip(j - n_lru, 0, n_rwkv - 1))),
                   pl.BlockSpec((tm, tn), lambda i, j: (i, jnp.maximum(j - n_lru - n_rwkv, 0)))],
        out_shape=[jax.ShapeDtypeStruct((t, LRU_COLS), PROJ_DTYPE),
                   jax.ShapeDtypeStruct((t, RWKV_COLS_PAD), PROJ_DTYPE),
                   jax.ShapeDtypeStruct((t, SSD_COLS_PAD), PROJ_DTYPE)],
        scratch_shapes=[pltpu.VMEM((tm, d), BF16)],
        compiler_params=pltpu.CompilerParams(
            dimension_semantics=("arbitrary", "arbitrary"), vmem_limit_bytes=VMEM_LIMIT),
        name="in_proj",
    )(x, g.reshape(1, d), w)


def _out_proj_kernel(x_ref, ya_ref, yb_ref, yc_ref, wa_ref, wb_ref, wc_ref, o_ref):
    acc = jnp.dot(_bf(ya_ref[...]), wa_ref[...], preferred_element_type=F32)
    acc += jnp.dot(_bf(yb_ref[...]), wb_ref[...], preferred_element_type=F32)
    acc += jnp.dot(_bf(yc_ref[...]), wc_ref[...], preferred_element_type=F32)
    o_ref[...] = x_ref[...] + acc


def _out_proj(x, y_lru, y_rwkv, y_ssd, wa, wb, wc):
    t, d = x.shape
    tm = min(MM_TM, t)
    tn = 1024
    return pl.pallas_call(
        _out_proj_kernel,
        grid=(t // tm, d // tn),
        in_specs=[pl.BlockSpec((tm, tn), lambda i, j: (i, j)),
                  pl.BlockSpec((tm, D_LRU), lambda i, j: (i, 0)),
                  pl.BlockSpec((tm, D_RWKV), lambda i, j: (i, 0)),
                  pl.BlockSpec((tm, D_SSD), lambda i, j: (i, 0)),
                  pl.BlockSpec((D_LRU, tn), lambda i, j: (0, j)),
                  pl.BlockSpec((D_RWKV, tn), lambda i, j: (0, j)),
                  pl.BlockSpec((D_SSD, tn), lambda i, j: (0, j))],
        out_specs=pl.BlockSpec((tm, tn), lambda i, j: (i, j)),
        out_shape=jax.ShapeDtypeStruct((t, d), F32),
        compiler_params=pltpu.CompilerParams(
            dimension_semantics=("parallel", "arbitrary"), vmem_limit_bytes=VMEM_LIMIT),
        name="out_proj",
    )(x, y_lru, y_rwkv, y_ssd, wa, wb, wc)


def _mlp_kernel(x_ref, g_ref, w1_ref, w2_ref, o_ref, u_ref):
    @pl.when(pl.program_id(1) == 0)
    def _():
        x = x_ref[...]
        u_ref[...] = _bf(_rms(x, g_ref[...]))
        o_ref[...] = x

    h = jnp.dot(u_ref[...], w1_ref[...], preferred_element_type=F32)
    h = jnp.square(jnp.maximum(h, 0.0))
    o_ref[...] += jnp.dot(_bf(h), w2_ref[...], preferred_element_type=F32)


def _mlp(x, g, w1, w2, layer):
    t, d = x.shape
    ff = w1.shape[2]
    tm = min(MM_TM, t)
    tf = MLP_TF
    return pl.pallas_call(
        _mlp_kernel,
        grid=(t // tm, ff // tf),
        in_specs=[pl.BlockSpec((tm, d), lambda i, f: (i, 0)),
                  pl.BlockSpec((1, d), lambda i, f: (0, 0)),
                  pl.BlockSpec((None, d, tf), lambda i, f: (layer, 0, f)),
                  pl.BlockSpec((None, tf, d), lambda i, f: (layer, f, 0))],
        out_specs=pl.BlockSpec((tm, d), lambda i, f: (i, 0)),
        out_shape=jax.ShapeDtypeStruct((t, d), F32),
        scratch_shapes=[pltpu.VMEM((tm, d), BF16)],
        compiler_params=pltpu.CompilerParams(
            dimension_semantics=("parallel", "arbitrary"), vmem_limit_bytes=VMEM_LIMIT),
        name="mlp",
    )(x, g.reshape(1, d), w1, w2)


def _final_norm_kernel(na, x_ref, g_ref, oa_ref, ob_ref):
    y = _rms(x_ref[...], g_ref[...])
    i = pl.program_id(0)

    @pl.when(i < na)
    def _():
        oa_ref[...] = y

    @pl.when(i >= na)
    def _():
        ob_ref[...] = y


def _final_norm(x, g, ta):
    t, d = x.shape
    tm = min(FINAL_TM, ta, t - ta)
    na = ta // tm
    return pl.pallas_call(
        functools.partial(_final_norm_kernel, na),
        grid=(t // tm,),
        in_specs=[pl.BlockSpec((tm, d), lambda i: (i, 0)),
                  pl.BlockSpec((1, d), lambda i: (0, 0))],
        out_specs=[pl.BlockSpec((tm, d), lambda i: (jnp.minimum(i, na - 1), 0)),
                   pl.BlockSpec((tm, d), lambda i: (jnp.maximum(i - na, 0), 0))],
        out_shape=[jax.ShapeDtypeStruct((ta, d), F32), jax.ShapeDtypeStruct((t - ta, d), F32)],
        compiler_params=pltpu.CompilerParams(dimension_semantics=("arbitrary",)),
        name="final_norm",
    )(x, g.reshape(1, d))


def _scan_rows(a, b, reverse):
    n = a.shape[0]
    row = lax.broadcasted_iota(jnp.int32, (n, 1), 0)
    s = 1
    while s < n:
        shift = n - s if reverse else s
        valid = (row < n - s) if reverse else (row >= s)
        a_sh = pltpu.roll(a, shift, 0)
        b_sh = pltpu.roll(b, shift, 0)
        b = jnp.where(valid, a * b_sh, 0.0) + b
        a = jnp.where(valid, a * a_sh, a)
        s *= 2
    return a, b


def _scan_block(a, b, carry, reverse):
    tiles = range(a.shape[0] // SUBLANE)
    loc = [_scan_rows(a[k * SUBLANE:(k + 1) * SUBLANE], b[k * SUBLANE:(k + 1) * SUBLANE], reverse) for k in tiles]
    hs = [None] * len(tiles)
    for k in (reversed(tiles) if reverse else tiles):
        a_cum, h_loc = loc[k]
        h = h_loc + a_cum * carry
        carry = h[0:1] if reverse else h[SUBLANE - 1:SUBLANE]
        hs[k] = h
    return jnp.concatenate(hs, axis=0), carry


def _lru_fwd_kernel(segs, tb, p_ref, pprev_ref, pnext_ref, cw_ref, cb_ref, wbd_ref, gb_ref, lam_ref,
                    hf_ref, a1_ref, bx1_ref, carry_ref):
    i = pl.program_id(0)
    pos, slen = _block_pos(i * tb, segs)
    is_start = pos == 0
    is_end = pos + tb == slen
    xb = p_ref[:, D_LRU:].astype(F32)
    prev_row, next0, next1 = _halo_rows(pprev_ref, pnext_ref, is_start, is_end, slice(D_LRU, None))
    xc = _conv4(xb, prev_row, next0, next1, cw_ref[...], cb_ref[...])
    gates = _sigmoid(_mm(xc, wbd_ref[...]) + gb_ref[...])
    sp = _softplus(-lam_ref[...])

    def direction(d):
        r = gates[:, (2 * d) * D_LRU:(2 * d + 1) * D_LRU]
        inp = gates[:, (2 * d + 1) * D_LRU:(2 * d + 2) * D_LRU]
        log_a = -LRU_C * r * sp[d:d + 1]
        th = jnp.tanh(log_a)
        one_minus_a2 = -2.0 * th / (1.0 - th)
        return jnp.exp(log_a), jnp.sqrt(one_minus_a2) * (inp * xc)

    a0, bx0 = direction(0)
    a1, bx1 = direction(1)
    a1_ref[...] = a1
    bx1_ref[...] = bx1

    @pl.when(is_start)
    def _():
        carry_ref[...] = jnp.zeros_like(carry_ref)

    h, carry = _scan_block(a0, bx0, carry_ref[...], False)
    hf_ref[...] = h
    carry_ref[...] = carry


def _lru_bwd_kernel(segs, tb, nb, p_ref, a1_ref, bx1_ref, hf_ref, o_ref, carry_ref):
    blk = nb - 1 - pl.program_id(0)
    pos, slen = _block_pos(blk * tb, segs)

    @pl.when(pos + tb == slen)
    def _():
        carry_ref[...] = jnp.zeros_like(carry_ref)

    h, carry = _scan_block(a1_ref[...], bx1_ref[...], carry_ref[...], True)
    carry_ref[...] = carry
    o_ref[...] = ((hf_ref[...] + h) * _gelu_tanh(p_ref[:, :D_LRU].astype(F32))).astype(o_ref.dtype)


def _lru_mixer(p, segs, conv_w, conv_b, w_bd, gate_b, lam):
    t = p.shape[0]
    tb = min(LRU_BLOCK, min(length for _, length in segs))
    nb = t // tb
    prev_spec, next_spec = _halo_specs(tb, LRU_COLS, t)
    const = lambda shape: pl.BlockSpec(shape, lambda i: (0,) * len(shape))
    row_spec = lambda w: pl.BlockSpec((tb, w), lambda i: (i, 0))
    hf, a1, bx1 = pl.pallas_call(
        functools.partial(_lru_fwd_kernel, segs, tb),
        grid=(nb,),
        in_specs=[row_spec(LRU_COLS), prev_spec, next_spec, const((4, D_LRU)), const((1, D_LRU)),
                  const((D_LRU, 4 * D_LRU)), const((1, 4 * D_LRU)), const((2, D_LRU))],
        out_specs=[row_spec(D_LRU)] * 3,
        out_shape=[jax.ShapeDtypeStruct((t, D_LRU), F32)] * 3,
        scratch_shapes=[pltpu.VMEM((1, D_LRU), F32)],
        compiler_params=pltpu.CompilerParams(
            dimension_semantics=("arbitrary",), vmem_limit_bytes=VMEM_LIMIT),
        name="lru_fwd",
    )(p, p, p, conv_w, conv_b.reshape(1, D_LRU), w_bd, gate_b.reshape(1, 4 * D_LRU), lam)
    rev_spec = lambda w: pl.BlockSpec((tb, w), lambda i: (nb - 1 - i, 0))
    return pl.pallas_call(
        functools.partial(_lru_bwd_kernel, segs, tb, nb),
        grid=(nb,),
        in_specs=[rev_spec(LRU_COLS), rev_spec(D_LRU), rev_spec(D_LRU), rev_spec(D_LRU)],
        out_specs=rev_spec(D_LRU),
        out_shape=jax.ShapeDtypeStruct((t, D_LRU), MIX_OUT_DTYPE),
        scratch_shapes=[pltpu.VMEM((1, D_LRU), F32)],
        compiler_params=pltpu.CompilerParams(
            dimension_semantics=("arbitrary",), vmem_limit_bytes=VMEM_LIMIT),
        name="lru_bwd",
    )(p, a1, bx1, hf)


def _ssd_core(reverse, xs, bm, cm, fac, u_t, hexp, ht_ref):
    q = xs.shape[0]
    off = SSD_HEADS if reverse else 0
    row = lax.broadcasted_iota(jnp.int32, (q, q), 0)
    col = lax.broadcasted_iota(jnp.int32, (q, q), 1)
    mask = (col >= row) if reverse else (col <= row)
    dt, u, out_fac, in_fac = (fac[:, n * LANE:(n + 1) * LANE] for n in range(4))
    chunk_decay = fac[0:1, 4 * LANE:5 * LANE]
    lane = lax.broadcasted_iota(jnp.int32, (1, LANE), 1)
    first = lane < HEAD_DIM
    heads = range(SSD_HEADS)
    pairs = range(SSD_HEADS // 2)
    grp = [h // (SSD_HEADS // SSD_GROUPS) for h in heads]
    cols = [slice(off + h, off + h + 1) for h in heads]
    gsl = [slice(g * SSD_STATE, (g + 1) * SSD_STATE) for g in range(SSD_GROUPS)]
    cm_g = [_bf(cm[:, sl]) for sl in gsl]
    bm_g = [_bf(bm[:, sl]) for sl in gsl]
    cb = [_mm_nt(cm_g[g], bm_g[g]) for g in range(SSD_GROUPS)]
    hts = [ht_ref[pr] for pr in pairs]
    hts_bf = [_bf(ht) for ht in hts]
    x_pair = [xs[:, pr * LANE:(pr + 1) * LANE] for pr in pairs]
    seg = [(u_t[c, :] - u[:, c]) if reverse else (u[:, c] - u_t[c, :]) for c in cols]
    m = [_bf(cb[grp[h]] * jnp.exp(jnp.where(mask, seg[h], -1e30))) for h in heads]
    dt_x = _mm_sel_rhs2(dt, hexp)
    dt_in_x = _mm_sel_rhs2(dt * in_fac, hexp)
    out_x = _mm_sel_rhs2(out_fac, hexp)
    psl = [slice(pr * LANE, (pr + 1) * LANE) for pr in pairs]
    xd = [_bf(x_pair[pr] * dt_x[:, psl[pr]]) for pr in pairs]
    xd_in = [_bf(x_pair[pr] * dt_in_x[:, psl[pr]]) for pr in pairs]
    y_diag = [_mm(m[h], xd[h // 2]) for h in heads]
    c_ht = {key: _mm(cm_g[key[1]], hts_bf[key[0]]) for key in sorted({(h // 2, grp[h]) for h in heads})}
    h_new = [chunk_decay[:, cols[h]] * hts[h // 2] + _mm_tn(bm_g[grp[h]], xd_in[h // 2]) for h in heads]
    for pr in pairs:
        ht_ref[pr] = jnp.where(first, h_new[2 * pr], h_new[2 * pr + 1])
    pick = lambda pr, vals: jnp.where(first, vals[0], vals[1])
    return [pick(pr, (y_diag[2 * pr], y_diag[2 * pr + 1]))
            + pick(pr, (c_ht[(pr, grp[2 * pr])], c_ht[(pr, grp[2 * pr + 1])])) * out_x[:, psl[pr]] for pr in pairs]


def _ssd_decay_kernel(q, dtraw_ref, dtb_ref, alog_ref, fac_ref, ut_ref):
    ks = range(dtraw_ref.shape[0] // q)
    rows = [slice(k * q, (k + 1) * q) for k in ks]
    lane = lax.broadcasted_iota(jnp.int32, (1, LANE), 1)
    fwd_lane = lane < SSD_HEADS
    a_head = jnp.where(lane < 2 * SSD_HEADS, -jnp.exp(alog_ref[...]), 0.0)
    row = lax.broadcasted_iota(jnp.int32, (q, q), 0)
    col = lax.broadcasted_iota(jnp.int32, (q, q), 1)
    lower = _bf((col <= row).astype(F32))
    dt = [_softplus(dtraw_ref[r, :].astype(F32) + dtb_ref[...]) for r in rows]
    da = [dt[k] * a_head for k in ks]
    cum = [_mm_sel_lhs(lower, da[k]) for k in ks]
    total = [cum[k][q - 1:q] for k in ks]
    u = [jnp.where(fwd_lane, cum[k], cum[k] - da[k]) for k in ks]
    e_u = [jnp.exp(u[k]) for k in ks]
    e_rest = [jnp.exp(total[k] - u[k]) for k in ks]
    for k in ks:
        pieces = (dt[k], u[k], jnp.where(fwd_lane, e_u[k], e_rest[k]), jnp.where(fwd_lane, e_rest[k], e_u[k]),
                  jnp.broadcast_to(jnp.exp(total[k]), (q, LANE)))
        for n, piece in enumerate(pieces):
            fac_ref[rows[k], n * LANE:(n + 1) * LANE] = piece
        ut_ref[:, rows[k]] = u[k].T


def _ssd_fwd_kernel(segs, q, p_ref, pprev_ref, pnext_ref, cw_ref, cb_ref, fac_ref, ut_ref, hexp_ref, dskip_ref,
                    xconv_ref, yacc_ref, ht_ref):
    i = pl.program_id(0)
    pos, slen = _block_pos(i * q, segs)
    is_start = pos == 0
    is_end = pos + q == slen
    prev_row, next0, next1 = _halo_rows(pprev_ref, pnext_ref, is_start, is_end)
    xbc = _silu(_conv4(p_ref[...].astype(F32), prev_row, next0, next1, cw_ref[...], cb_ref[...]))
    xconv_ref[...] = xbc
    xs = xbc[:, :D_SSD]

    @pl.when(is_start)
    def _():
        ht_ref[...] = jnp.zeros_like(ht_ref)

    ys = _ssd_core(False, xs, xbc[:, D_SSD:D_SSD + SSD_BC], xbc[:, D_SSD + SSD_BC:], fac_ref[...], ut_ref[...],
                   hexp_ref[...], ht_ref)
    for pr, y in enumerate(ys):
        sl = slice(pr * LANE, (pr + 1) * LANE)
        yacc_ref[:, sl] = y + dskip_ref[:, sl] * xs[:, sl]


def _ssd_bwd_kernel(segs, q, nb, z_ref, xconv_ref, yacc_ref, fac_ref, ut_ref, hexp_ref, ng_ref, o_ref, ht_ref):
    blk = nb - 1 - pl.program_id(0)
    pos, slen = _block_pos(blk * q, segs)

    @pl.when(pos + q == slen)
    def _():
        ht_ref[...] = jnp.zeros_like(ht_ref)

    xbc = xconv_ref[...]
    ys = _ssd_core(True, xbc[:, :D_SSD], xbc[:, D_SSD:D_SSD + SSD_BC], xbc[:, D_SSD + SSD_BC:], fac_ref[...],
                   ut_ref[...], hexp_ref[...], ht_ref)
    y = jnp.concatenate(ys, axis=1) + yacc_ref[...]
    o_ref[...] = _rms(y * _silu(z_ref[...].astype(F32)), ng_ref[...]).astype(o_ref.dtype)


def _ssd_mixer(p, segs, conv_w, conv_b, dt_bias, a_log, d_skip, norm_g):
    t = p.shape[0]
    q = SSD_CHUNK
    nb = t // q
    prev_spec, next_spec = _halo_specs(q, SSD_CONV_DIM, t)
    const = lambda shape: pl.BlockSpec(shape, lambda i: (0,) * len(shape))
    row_spec = lambda w: pl.BlockSpec((q, w), lambda i: (i, 0))
    state = pltpu.VMEM((SSD_HEADS // 2, SSD_STATE, LANE), F32)
    rows = min(SSD_DECAY_ROWS, t)
    dt_col = SSD_DT_OFF // LANE
    head_expand = lambda off: (jnp.arange(LANE)[:, None] == off + jnp.arange(D_SSD)[None, :] // HEAD_DIM).astype(BF16)
    fac, u_t = pl.pallas_call(
        functools.partial(_ssd_decay_kernel, q),
        grid=(t // rows,),
        in_specs=[pl.BlockSpec((rows, LANE), lambda i: (i, dt_col)), const((1, LANE)), const((1, LANE))],
        out_specs=[pl.BlockSpec((rows, SSD_FAC_COLS), lambda i: (i, 0)), pl.BlockSpec((LANE, rows), lambda i: (0, i))],
        out_shape=[jax.ShapeDtypeStruct((t, SSD_FAC_COLS), F32), jax.ShapeDtypeStruct((LANE, t), F32)],
        compiler_params=pltpu.CompilerParams(
            dimension_semantics=("parallel",), vmem_limit_bytes=VMEM_LIMIT),
        name="ssd_decay",
    )(p, dt_bias, a_log)
    xconv, yacc = pl.pallas_call(
        functools.partial(_ssd_fwd_kernel, segs, q),
        grid=(nb,),
        in_specs=[row_spec(SSD_CONV_DIM), prev_spec, next_spec, const((4, SSD_CONV_DIM)),
                  const((1, SSD_CONV_DIM)), row_spec(SSD_FAC_COLS), pl.BlockSpec((LANE, q), lambda i: (0, i)),
                  const((LANE, D_SSD)), const((1, D_SSD))],
        out_specs=[row_spec(SSD_CONV_DIM), row_spec(D_SSD)],
        out_shape=[jax.ShapeDtypeStruct((t, SSD_CONV_DIM), F32), jax.ShapeDtypeStruct((t, D_SSD), F32)],
        scratch_shapes=[state],
        compiler_params=pltpu.CompilerParams(
            dimension_semantics=("arbitrary",), vmem_limit_bytes=VMEM_LIMIT),
        name="ssd_fwd",
    )(p, p, p, conv_w, conv_b.reshape(1, SSD_CONV_DIM), fac, u_t, head_expand(0), d_skip)
    rev_spec = lambda w: pl.BlockSpec((q, w), lambda i: (nb - 1 - i, 0))
    return pl.pallas_call(
        functools.partial(_ssd_bwd_kernel, segs, q, nb),
        grid=(nb,),
        in_specs=[pl.BlockSpec((q, D_SSD), lambda i: (nb - 1 - i, SSD_Z_OFF // D_SSD)),
                  rev_spec(SSD_CONV_DIM), rev_spec(D_SSD), rev_spec(SSD_FAC_COLS),
                  pl.BlockSpec((LANE, q), lambda i: (0, nb - 1 - i)), const((LANE, D_SSD)), const((1, D_SSD))],
        out_specs=rev_spec(D_SSD),
        out_shape=jax.ShapeDtypeStruct((t, D_SSD), MIX_OUT_DTYPE),
        scratch_shapes=[state],
        compiler_params=pltpu.CompilerParams(
            dimension_semantics=("arbitrary",), vmem_limit_bytes=VMEM_LIMIT),
        name="ssd_bwd",
    )(p, xconv, yacc, fac, u_t, head_expand(SSD_HEADS), norm_g.reshape(1, D_SSD))


def _rwkv_pre_kernel(segs, tb, p_ref, pprev_ref, pnext_ref, mu_ref, ww_ref, wa_ref, w0a0_ref, gup_ref, kk_ref_w,
                     ka_ref, rk_ref, hsel_ref, hselt_ref,
                     r_out, v_out, kk_out, lw0_out, lw1_out, kd0_out, kd1_out, b0_out, b1_out, g_out, bon_out):
    i = pl.program_id(0)
    pos, slen = _block_pos(i * tb, segs)
    p = p_ref[...].astype(F32)
    prev_row, next_row, _ = _halo_rows(pprev_ref, pnext_ref, pos == 0, pos + tb == slen)
    pm1, pp1, _ = _shifted(p, prev_row, next_row, None)
    mu = mu_ref[...]
    ps = p + mu[0:1] * (pm1 - p) + mu[1:2] * (pp1 - p)
    c = D_RWKV
    r = ps[:, 0:c]
    k = ps[:, c:2 * c]
    v = ps[:, 2 * c:3 * c]
    lo0 = 3 * c
    w_win = ps[:, lo0:lo0 + RWKV_LO_WIN]
    a_win = ps[:, lo0 + 4 * RWKV_RANK - RWKV_LO_WIN:lo0 + 4 * RWKV_RANK]
    g_lo = ps[:, lo0 + 4 * RWKV_RANK:lo0 + 4 * RWKV_RANK + RWKV_GATE_RANK]
    pre_w = _mm(jnp.tanh(w_win), ww_ref[...]) + w0a0_ref[:, :2 * c]
    pre_a = _mm(a_win, wa_ref[...]) + w0a0_ref[:, 2 * c:]
    g = _mm(_sigmoid(g_lo), gup_ref[...])
    hsel = hsel_ref[...]
    hselt = hselt_ref[...]
    kk = k * kk_ref_w[...]
    ss = _head_sum(kk * kk, hsel, hselt)
    kk = kk / jnp.maximum(jnp.sqrt(ss), 1e-12)
    kd_sum = None
    outs = ((lw0_out, kd0_out, b0_out), (lw1_out, kd1_out, b1_out))
    for d in range(2):
        w_pre = pre_w[:, d * c:(d + 1) * c]
        a = _sigmoid(pre_a[:, d * c:(d + 1) * c])
        outs[d][0][...] = -math.exp(-0.5) * _sigmoid(w_pre)
        kd = k * (1.0 + (a - 1.0) * ka_ref[...])
        outs[d][1][...] = kd.astype(outs[d][1].dtype)
        outs[d][2][...] = (kk * a).astype(outs[d][2].dtype)
        kd_sum = kd if kd_sum is None else kd_sum + kd
    r_out[...] = r.astype(r_out.dtype)
    v_out[...] = v.astype(v_out.dtype)
    kk_out[...] = kk.astype(kk_out.dtype)
    g_out[...] = g
    bon_out[...] = _head_sum(r * kd_sum * rk_ref[...], hsel, hselt) * v


def _rwkv_chunk_problems(reverse, cl, r_ref, v_ref, kk_ref, lw_ref, kd_ref, b_ref):
    n2 = 2 * cl
    rowc = lax.broadcasted_iota(jnp.int32, (cl, cl), 0)
    colc = lax.broadcasted_iota(jnp.int32, (cl, cl), 1)
    tri = _bf(((colc >= rowc) if reverse else (colc <= rowc)).astype(F32))
    lw = lw_ref[...]
    cum = _mm_sel_lhs(tri, lw)
    total = cum[0:1] if reverse else cum[cl - 1:cl]
    g_in = jnp.exp(cum)
    g_prev = jnp.exp(cum - lw)
    g_inv = jnp.exp(-cum)
    g_end = jnp.exp(total - cum)
    g_tot = jnp.exp(total)
    kk = kk_ref[...]
    zt = -kk * g_prev
    rt = r_ref[...] * g_in
    bh = b_ref[...] * g_inv
    kh = kd_ref[...] * g_inv
    bg = b_ref[...] * g_end
    kg = kd_ref[...] * g_end
    v = v_ref[...]

    row = lax.broadcasted_iota(jnp.int32, (n2, n2), 0)
    col = lax.broadcasted_iota(jnp.int32, (n2, n2), 1)
    same = (row >= cl) == (col >= cl)
    rt_i = jnp.where(row >= cl, row - cl, row)
    ct_i = jnp.where(col >= cl, col - cl, col)
    if reverse:
        m_strict = same & (ct_i > rt_i)
        m_incl = same & (ct_i >= rt_i)
    else:
        m_strict = same & (ct_i < rt_i)
        m_incl = same & (ct_i <= rt_i)
    eye = row == col
    first = lax.broadcasted_iota(jnp.int32, (1, LANE), 1) < HEAD_DIM
    pair_mask = same & ((rt_i >> 1) == (ct_i >> 1))
    join_masks = []
    shift = 1
    while (1 << shift) < cl:
        later, earlier = (ct_i, rt_i) if reverse else (rt_i, ct_i)
        join_masks.append(same & ((rt_i >> (shift + 1)) == (ct_i >> (shift + 1)))
                          & (((later >> shift) & 1) == 1) & (((earlier >> shift) & 1) == 0))
        shift += 1

    def stack(x):
        return jnp.concatenate([jnp.where(first, x, 0.0), jnp.where(first, 0.0, x)], axis=0)

    masks = dict(strict=m_strict, incl=m_incl, eye=eye, pair=pair_mask, joins=join_masks)
    problems = []
    for pr in range(D_RWKV // LANE):
        sl = slice(pr * LANE, (pr + 1) * LANE)
        ztm, rtm, bhm, khm, bgm, kgm, vm = (_bf(stack(x[:, sl])) for x in (zt, rt, bh, kh, bg, kg, v))
        problems.append(dict(masks, ztm=ztm, rtm=rtm, bhm=bhm, khm=khm, bgm=bgm, kgm=kgm, vm=vm,
                             g_tot=g_tot[:, sl]))
    return problems


def _rwkv_solve(problems, sts):
    ps = range(len(problems))
    n2 = problems[0]['ztm'].shape[0]
    q = problems
    zr = [jnp.concatenate([q[i]['ztm'], q[i]['rtm']], axis=0) for i in ps]
    bk = [jnp.concatenate([q[i]['bhm'], q[i]['khm']], axis=0) for i in ps]
    g = [_mm_nt(zr[i], bk[i]) for i in ps]
    a_ab = [jnp.where(q[i]['strict'], g[i][:n2, :n2], 0.0) for i in ps]
    a_ak = [_bf(jnp.where(q[i]['strict'], g[i][:n2, n2:], 0.0)) for i in ps]
    a_rb = [_bf(jnp.where(q[i]['incl'], g[i][n2:, :n2], 0.0)) for i in ps]
    a_rk = [_bf(jnp.where(q[i]['incl'], g[i][n2:, n2:], 0.0)) for i in ps]
    sa_v = [_mm(a_ak[i], q[i]['vm']) for i in ps]
    tinv = [jnp.where(q[i]['eye'], 1.0, jnp.where(q[i]['pair'], a_ab[i], 0.0)) for i in ps]
    for level in range(len(q[0]['joins'])):
        tb16 = [_bf(tinv[i]) for i in ps]
        half = [_mm(tb16[i], jnp.where(q[i]['joins'][level], a_ab[i], 0.0)) for i in ps]
        tinv = [tinv[i] + _mm(half[i], tb16[i]) for i in ps]
    x = [_bf(_mm(tinv[i], jnp.concatenate([q[i]['ztm'], _bf(sa_v[i])], axis=1))) for i in ps]
    z_eff = [x[i][:, :LANE] for i in ps]
    bg_t = [q[i]['bgm'].T for i in ps]
    kg_t = [q[i]['kgm'].T for i in ps]
    r_eff = [_bf(q[i]['rtm'].astype(F32) + _mm(a_rb[i], z_eff[i])) for i in ps]
    p_mat = [_bf(jnp.where(q[i]['eye'], q[i]['g_tot'], 0.0) + _mm(bg_t[i], z_eff[i])) for i in ps]
    rhs = [jnp.concatenate([x[i][:, LANE:], q[i]['vm'], _bf(sts[i])], axis=0) for i in ps]
    lhs = [jnp.concatenate([jnp.concatenate([a_rb[i], a_rk[i], r_eff[i]], axis=1),
                            jnp.concatenate([bg_t[i], kg_t[i], p_mat[i]], axis=1)], axis=0) for i in ps]
    out = [_mm(lhs[i], rhs[i]) for i in ps]
    return [out[i][:n2] for i in ps], [out[i][n2:] for i in ps]


def _rwkv_scan_kernel(segs, cl, nb, rf_ref, vf_ref, kkf_ref, lwf_ref, kdf_ref, bf_ref,
                      rb_ref, vb_ref, kkb_ref, lwb_ref, kdb_ref, bb_ref, yf_ref, yb_ref, stf_ref, stb_ref):
    i = pl.program_id(0)
    pos_f, _ = _block_pos(i * cl, segs)
    pos_b, slen_b = _block_pos((nb - 1 - i) * cl, segs)

    @pl.when(pos_f == 0)
    def _():
        stf_ref[...] = jnp.zeros_like(stf_ref)

    @pl.when(pos_b + cl == slen_b)
    def _():
        stb_ref[...] = jnp.zeros_like(stb_ref)

    npair = D_RWKV // LANE
    problems = (_rwkv_chunk_problems(False, cl, rf_ref, vf_ref, kkf_ref, lwf_ref, kdf_ref, bf_ref)
                + _rwkv_chunk_problems(True, cl, rb_ref, vb_ref, kkb_ref, lwb_ref, kdb_ref, bb_ref))
    sts = [stf_ref[pr] for pr in range(npair)] + [stb_ref[pr] for pr in range(npair)]
    ym, st_new = _rwkv_solve(problems, sts)
    for pr in range(npair):
        sl = slice(pr * LANE, (pr + 1) * LANE)
        stf_ref[pr] = st_new[pr]
        stb_ref[pr] = st_new[npair + pr]
        yf_ref[:, sl] = ym[pr][:cl] + ym[pr][cl:]
        yb_ref[:, sl] = ym[npair + pr][:cl] + ym[npair + pr][cl:]


def _rwkv_post_kernel(yf_ref, yb_ref, bon_ref, g_ref, lng_ref, lnb_ref, hsel_ref, hselt_ref, o_ref):
    y = yf_ref[...] + yb_ref[...]
    hsel = hsel_ref[...]
    hselt = hselt_ref[...]
    mean = _head_sum(y, hsel, hselt) * (1.0 / HEAD_DIM)
    d = y - mean
    var = _head_sum(d * d, hsel, hselt) * (1.0 / HEAD_DIM)
    y = d * lax.rsqrt(var + RWKV_GN_EPS) * lng_ref[...] + lnb_ref[...]
    o_ref[...] = ((y + bon_ref[...]) * g_ref[...]).astype(o_ref.dtype)


def _rwkv_mixer(p, segs, mu, w_w, w_a, w0a0, g_up, k_k, k_a, r_k, ln_g, ln_b, hsel):
    hselt = hsel.T
    t = p.shape[0]
    c = D_RWKV
    tb = min(TOK_BLOCK, min(length for _, length in segs))
    prev_spec, next_spec = _halo_specs(tb, RWKV_COLS_PAD, t)
    const = lambda shape: pl.BlockSpec(shape, lambda i: (0,) * len(shape))
    row_spec = lambda n, w: pl.BlockSpec((n, w), lambda i: (i, 0))
    vec = lambda a: a.reshape(1, c)
    tok = jax.ShapeDtypeStruct((t, c), F32)
    opd = jax.ShapeDtypeStruct((t, c), BF16)
    r, v, kk, lw0, lw1, kd0, kd1, b0, b1, g, bon = pl.pallas_call(
        functools.partial(_rwkv_pre_kernel, segs, tb),
        grid=(t // tb,),
        in_specs=[row_spec(tb, RWKV_COLS_PAD), prev_spec, next_spec, const((2, RWKV_COLS_PAD)),
                  const((RWKV_LO_WIN, 2 * c)), const((RWKV_LO_WIN, 2 * c)), const((1, 4 * c)),
                  const((RWKV_GATE_RANK, c)), const((1, c)), const((1, c)), const((1, c)),
                  const((c, LANE)), const((LANE, c))],
        out_specs=[row_spec(tb, c)] * 11,
        out_shape=[opd, opd, opd, tok, tok, opd, opd, opd, opd, tok, tok],
        compiler_params=pltpu.CompilerParams(
            dimension_semantics=("parallel",), vmem_limit_bytes=VMEM_LIMIT),
        name="rwkv_pre",
    )(p, p, p, mu, w_w, w_a, w0a0, g_up, vec(k_k), vec(k_a), vec(r_k), hsel, hselt)

    cl = RWKV_CHUNK
    nb = t // cl
    fwd_spec = pl.BlockSpec((cl, c), lambda i: (i, 0))
    bwd_spec = pl.BlockSpec((cl, c), lambda i: (nb - 1 - i, 0))
    state = pltpu.VMEM((c // LANE, LANE, LANE), F32)
    ys = pl.pallas_call(
        functools.partial(_rwkv_scan_kernel, segs, cl, nb),
        grid=(nb,),
        in_specs=[fwd_spec] * 6 + [bwd_spec] * 6,
        out_specs=[fwd_spec, bwd_spec],
        out_shape=[tok, tok],
        scratch_shapes=[state, state],
        compiler_params=pltpu.CompilerParams(
            dimension_semantics=("arbitrary",), vmem_limit_bytes=VMEM_LIMIT),
        name="rwkv_scan",
    )(r, v, kk, lw0, kd0, b0, r, v, kk, lw1, kd1, b1)

    return pl.pallas_call(
        _rwkv_post_kernel,
        grid=(t // tb,),
        in_specs=[row_spec(tb, c)] * 4 + [const((1, c)), const((1, c)), const((c, LANE)), const((LANE, c))],
        out_specs=row_spec(tb, c),
        out_shape=jax.ShapeDtypeStruct((t, c), MIX_OUT_DTYPE),
        compiler_params=pltpu.CompilerParams(
            dimension_semantics=("parallel",), vmem_limit_bytes=VMEM_LIMIT),
        name="rwkv_post",
    )(ys[0], ys[1], bon, g, vec(ln_g), vec(ln_b), hsel, hselt)


def _pad_cols(w, n):
    return jnp.pad(w, ((0, 0), (0, n - w.shape[1])))


def _lru_gate_blockdiag(gate_w):
    nh = gate_w.shape[2]
    eye = jnp.eye(nh, dtype=gate_w.dtype)
    w = jnp.einsum('dghij,hk->hidgkj', gate_w, eye)
    return w.reshape(D_LRU, 4 * D_LRU)


def _rwkv_lowrank_blocks(up, row0):
    out = jnp.zeros((RWKV_LO_WIN, 2 * D_RWKV), up.dtype)
    for d in range(2):
        out = lax.dynamic_update_slice(out, up[d], (row0 + d * RWKV_RANK, d * D_RWKV))
    return out


def _encoder(x, segs, w):
    (norm1_g, w_in, lru_conv_w, lru_conv_b, lru_gate_w, lru_gate_b, lru_lambda,
     rwkv_mu, rwkv_w0, rwkv_w_up, rwkv_a0, rwkv_a_up, rwkv_g_up, rwkv_k_k, rwkv_k_a,
     rwkv_r_k, rwkv_ln_g, rwkv_ln_b, ssd_conv_w, ssd_conv_b, ssd_dt_bias, ssd_a_log,
     ssd_d, ssd_norm_g, w_out, norm2_g, mlp_w1, mlp_w2, final_norm_g) = w
    depth = w_in.shape[0]
    head_id = jnp.arange(D_RWKV) // HEAD_DIM
    hsel = (head_id[:, None] == jnp.arange(LANE)[None, :]).astype(BF16)
    o1 = LRU_COLS
    o2 = LRU_COLS + RWKV_COLS
    pad_last = lambda a, n: jnp.pad(a, ((0, 0), (0, 0), (0, n - a.shape[2])))
    w_in_bf = w_in.astype(BF16)
    w_in_pad = jnp.concatenate([w_in_bf[:, :, :o1], pad_last(w_in_bf[:, :, o1:o2], RWKV_COLS_PAD),
                                pad_last(w_in_bf[:, :, o2 + D_SSD:], SSD_Z_OFF), w_in_bf[:, :, o2:o2 + D_SSD]],
                               axis=2)
    w_in_pad = w_in_pad.reshape(depth, D_MODEL, -1, IN_PROJ_TN).transpose(0, 2, 1, 3)
    mlp_w1_bf = mlp_w1.astype(BF16)
    mlp_w2_bf = mlp_w2.astype(BF16)
    for l in range(depth):
        p_lru, p_rwkv, p_ssd = _in_proj(x, norm1_g[l], w_in_pad, l)

        y_lru = _lru_mixer(p_lru, segs, lru_conv_w[l], lru_conv_b[l],
                           _lru_gate_blockdiag(lru_gate_w[l]).astype(BF16), lru_gate_b[l], lru_lambda[l])

        y_rwkv = _rwkv_mixer(
            p_rwkv, segs, _pad_cols(rwkv_mu[l], RWKV_COLS_PAD),
            _rwkv_lowrank_blocks(rwkv_w_up[l], 0).astype(BF16),
            _rwkv_lowrank_blocks(rwkv_a_up[l], RWKV_LO_WIN - 2 * RWKV_RANK).astype(BF16),
            jnp.concatenate([rwkv_w0[l, 0], rwkv_w0[l, 1], rwkv_a0[l, 0], rwkv_a0[l, 1]]).reshape(1, 4 * D_RWKV),
            rwkv_g_up[l].astype(BF16), rwkv_k_k[l], rwkv_k_a[l], rwkv_r_k[l], rwkv_ln_g[l], rwkv_ln_b[l], hsel)

        y_ssd = _ssd_mixer(
            p_ssd, segs, ssd_conv_w[l], ssd_conv_b[l],
            _pad_cols(ssd_dt_bias[l].reshape(1, 2 * SSD_HEADS), LANE),
            _pad_cols(ssd_a_log[l].reshape(1, 2 * SSD_HEADS), LANE),
            jnp.repeat(ssd_d[l], HEAD_DIM).reshape(1, D_SSD), ssd_norm_g[l])

        w_out_l = w_out[l].astype(BF16)
        x = _out_proj(x, y_lru, y_rwkv, y_ssd, w_out_l[:D_LRU], w_out_l[D_LRU:D_LRU + D_RWKV],
                      w_out_l[D_LRU + D_RWKV:])
        x = _mlp(x, norm2_g[l], mlp_w1_bf, mlp_w2_bf, l)
    return _final_norm(x, final_norm_g, segs[0][0] * segs[0][1])


def kernel(x_prompt, x_sample, norm1_g, w_in, lru_conv_w, lru_conv_b, lru_gate_w, lru_gate_b, lru_lambda, rwkv_mu, rwkv_w0, rwkv_w_up, rwkv_a0, rwkv_a_up, rwkv_g_up, rwkv_k_k, rwkv_k_a, rwkv_r_k, rwkv_ln_g, rwkv_ln_b, ssd_conv_w, ssd_conv_b, ssd_dt_bias, ssd_a_log, ssd_d, ssd_norm_g, w_out, norm2_g, mlp_w1, mlp_w2, final_norm_g):
    weights = (norm1_g, w_in, lru_conv_w, lru_conv_b, lru_gate_w, lru_gate_b, lru_lambda,
               rwkv_mu, rwkv_w0, rwkv_w_up, rwkv_a0, rwkv_a_up, rwkv_g_up, rwkv_k_k, rwkv_k_a,
               rwkv_r_k, rwkv_ln_g, rwkv_ln_b, ssd_conv_w, ssd_conv_b, ssd_dt_bias, ssd_a_log,
               ssd_d, ssd_norm_g, w_out, norm2_g, mlp_w1, mlp_w2, final_norm_g)
    bp, lp, d = x_prompt.shape
    bs, ls, _ = x_sample.shape
    segs = ((bp, lp), (bs, ls))
    x = jnp.concatenate([x_prompt.reshape(bp * lp, d), x_sample.reshape(bs * ls, d)], axis=0)
    y_prompt, y_sample = _encoder(x, segs, weights)
    return y_prompt.reshape(bp, lp, d), y_sample.reshape(bs, ls, d)
```
